```python
import jax, jax.numpy as jnp
from jax import lax
import numpy as np

D_MODEL = 2048
BATCH = 2
SEQ = 4096
DEPTH = 4
DEC_BATCH = 8
DEC_SEQ = 8
PAST_LEN = 16384
PAGE_SIZE = 128

N_A_LAYERS = DEPTH // 2
N_B_LAYERS = DEPTH - N_A_LAYERS
HEAD_DIM = 128
N_HEADS = D_MODEL // HEAD_DIM
N_KV = 4
GROUP = N_HEADS // N_KV
D_FF = ((8 * D_MODEL // 3 + 255) // 256) * 256
CONV_W = 31
BLK = 64
TOP_N = 16
WINDOW = 512
WIN_QBLK = 128
SEL_QCHUNK = 32
CMP_HID = 2 * HEAD_DIM
EPS = 1e-6
NEG = -1e30
FORCE = 1e4
HALF = 0.5
SCALE = HEAD_DIM ** -0.5

kernel_name = 'conformer_conv_yoco_nsa_decoder_step'


def _rms(x, g):
    xf = x.astype(jnp.float32)
    y = xf * lax.rsqrt(jnp.mean(xf * xf, axis=-1, keepdims=True) + EPS)
    return (y * g.astype(jnp.float32)).astype(x.dtype)


def _layernorm(x, g, b):
    xf = x.astype(jnp.float32)
    mu = jnp.mean(xf, axis=-1, keepdims=True)
    var = jnp.mean(jnp.square(xf - mu), axis=-1, keepdims=True)
    return ((xf - mu) * lax.rsqrt(var + EPS) * g.astype(jnp.float32) + b.astype(jnp.float32)).astype(x.dtype)


def _modulate(x, g, shift, scale):
    return _rms(x, g) * (1 + scale[:, None, :]) + shift[:, None, :]


def _swiglu(h, w_in, w_out):
    a, b = jnp.split(h @ w_in, 2, axis=-1)
    return (jax.nn.silu(a) * b) @ w_out


def _masked_softmax(s, valid):
    s = jnp.where(valid, s, NEG)
    m = jnp.max(s, axis=-1, keepdims=True)
    e = jnp.where(valid, jnp.exp(s - m), 0.0)
    return e / jnp.maximum(jnp.sum(e, axis=-1, keepdims=True), 1e-30)


def _alibi_slopes():
    m = np.exp2(-8.0 * np.arange(1, N_HEADS + 1) / N_HEADS)
    return jnp.asarray(m, dtype=jnp.float32).reshape(N_KV, GROUP)


def _conv_module(h, hist, w_in, b_in, dw, dw_b, ln_g, ln_b, w_out, b_out):
    a, g = jnp.split(h @ w_in + b_in, 2, axis=-1)
    u = a * jax.nn.sigmoid(g)
    u_ext = jnp.concatenate([hist, u], axis=1)
    y = lax.conv_general_dilated(u_ext, dw[:, None, :], window_strides=(1,), padding='VALID',
                                 dimension_numbers=('NWC', 'WIO', 'NWC'),
                                 feature_group_count=D_MODEL) + dw_b
    y = jax.nn.silu(_layernorm(y, ln_g, ln_b))
    return y @ w_out + b_out, u_ext[:, -(CONV_W - 1):]


def _compress(kv2, cmp_pos, cmp_w1, cmp_b1, cmp_w2, kn_g):
    B, T = kv2.shape[0], kv2.shape[1]
    nc = T // BLK
    blocks = kv2[:, :nc * BLK].reshape(B, nc, BLK, 2, N_KV, HEAD_DIM) + cmp_pos[None, None, :, :, None, :]
    flat = blocks.transpose(0, 1, 3, 4, 2, 5).reshape(B, nc, 2, N_KV, BLK * HEAD_DIM)
    hid = jax.nn.gelu(jnp.einsum('bnpgf,pfh->bnpgh', flat, cmp_w1) + cmp_b1[:, None, :])
    out = jnp.einsum('bnpgh,phd->bnpgd', hid, cmp_w2)
    return _rms(out[:, :, 0], kn_g), out[:, :, 1]


def _selection_blocks(kv2):
    B, T = kv2.shape[0], kv2.shape[1]
    nsel = -(-T // BLK)
    kv2 = jnp.pad(kv2, ((0, 0), (0, nsel * BLK - T), (0, 0), (0, 0), (0, 0)))
    return kv2.reshape(B, nsel, BLK, 2, N_KV, HEAD_DIM).transpose(0, 4, 1, 2, 3, 5)


def _cmp_branch(q, qpos, kc, vc, slopes):
    nc = kc.shape[1]
    kpos = jnp.arange(nc, dtype=jnp.int32) * BLK + (BLK - 1)
    dist = (qpos[:, None] - kpos[None, :]).astype(jnp.float32)
    s = jnp.einsum('btgrd,bngd->btgrn', q, kc).astype(jnp.float32) * SCALE
    s = s - slopes[None, None, :, :, None] * dist[None, :, None, None, :]
    p = _masked_softmax(s, (dist >= 0)[None, :, None, None, :])
    return jnp.einsum('btgrn,bngd->btgrd', p.astype(vc.dtype), vc), p


def _sel_branch(q, qpos, p_cmp, sel_blocks, slopes):
    B, T = q.shape[0], q.shape[1]
    nsel = sel_blocks.shape[2]
    nc = p_cmp.shape[-1]
    imp = jnp.pad(jnp.sum(p_cmp, axis=3), ((0, 0), (0, 0), (0, 0), (0, nsel - nc)))
    blk = jnp.arange(nsel, dtype=jnp.int32)[None, None, :]
    cur = (qpos // BLK)[:, None, None]
    forced = (blk == 0) | (blk == cur) | (blk == cur - 1)
    score = jnp.where(forced, FORCE, imp)
    score = jnp.where(blk <= cur, score, -1.0)
    vals, idx = lax.top_k(score, min(TOP_N, nsel))
    ok = vals >= 0
    k = idx.shape[-1]
    cq = min(SEL_QCHUNK, T)
    if T % cq:
        cq = T
    n_ch = T // cq
    b_ix = jnp.arange(B)[:, None, None, None]
    g_ix = jnp.arange(N_KV)[None, None, :, None]

    def one(args):
        qc, qp, ic, okc = args
        kv = sel_blocks[b_ix, g_ix, ic]
        kpos = ic[..., None] * BLK + jnp.arange(BLK, dtype=jnp.int32)
        dist = (qp[None, :, None, None, None] - kpos).astype(jnp.float32)
        valid = (okc[..., None] & (dist >= 0)).reshape(B, cq, N_KV, 1, k * BLK)
        dist = dist.reshape(B, cq, N_KV, 1, k * BLK)
        s = jnp.einsum('bqgrd,bqgnjd->bqgrnj', qc, kv[..., 0, :]).astype(jnp.float32) * SCALE
        s = s.reshape(B, cq, N_KV, GROUP, k * BLK) - slopes[None, None, :, :, None] * dist
        p = _masked_softmax(s, valid)
        v = kv[..., 1, :].reshape(B, cq, N_KV, k * BLK, HEAD_DIM)
        return jnp.einsum('bqgrm,bqgmd->bqgrd', p.astype(v.dtype), v)

    def split(a):
        return jnp.swapaxes(a.reshape((a.shape[0], n_ch, cq) + a.shape[2:]), 0, 1)

    out = lax.map(one, (split(q), qpos.reshape(n_ch, cq), split(idx), split(ok)))
    return jnp.swapaxes(out, 0, 1).reshape(q.shape)


def _win_branch(q, qpos, win_all, banded, slopes):
    B, T = q.shape[0], q.shape[1]
    if banded:
        lq = WIN_QBLK if T % WIN_QBLK == 0 else T
        nb = T // lq
        kv_pad = jnp.pad(win_all, ((0, 0), (WINDOW, 0), (0, 0), (0, 0), (0, 0)))
        idx = (jnp.arange(nb, dtype=jnp.int32) * lq)[:, None] + jnp.arange(WINDOW + lq, dtype=jnp.int32)[None, :]
        kv = kv_pad[:, idx]
        kpos = idx - WINDOW
    else:
        lq, nb = T, 1
        lw = win_all.shape[1]
        kv = win_all[:, None]
        kpos = ((qpos[-1] + 1 - lw) + jnp.arange(lw, dtype=jnp.int32))[None, :]
    qb = q.reshape(B, nb, lq, N_KV, GROUP, HEAD_DIM)
    qp = qpos.reshape(nb, lq)
    dist = (qp[:, :, None] - kpos[:, None, :]).astype(jnp.float32)
    valid = (dist >= 0) & (dist < WINDOW) & (kpos >= 0)[:, None, :]
    s = jnp.einsum('bnqgrd,bnkgd->bnqgrk', qb, kv[:, :, :, 0]).astype(jnp.float32) * SCALE
    s = s - slopes[None, None, None, :, :, None] * dist[None, :, :, None, None, :]
    p = _masked_softmax(s, valid[None, :, :, None, None, :])
    o = jnp.einsum('bnqgrk,bnkgd->bnqgrd', p.astype(kv.dtype), kv[:, :, :, 1])
    return o.reshape(q.shape)


def _nsa(h, qpos, shared, w_qg, q_norm_g, w_o):
    kc, vc, sel_blocks, win_all, banded = shared
    B, T = h.shape[0], h.shape[1]
    hd = N_HEADS * HEAD_DIM
    proj = h @ w_qg
    q = _rms(proj[..., :hd].reshape(B, T, N_KV, GROUP, HEAD_DIM), q_norm_g)
    gates = jax.nn.sigmoid(proj[..., hd:].astype(jnp.float32)).reshape(B, T, 3, N_KV, GROUP, 1)
    slopes = _alibi_slopes()
    o_cmp, p_cmp = _cmp_branch(q, qpos, kc, vc, slopes)
    o_sel = _sel_branch(q, qpos, p_cmp, sel_blocks, slopes)
    o_win = _win_branch(q, qpos, win_all, banded, slopes)
    o = gates[:, :, 0] * o_cmp + gates[:, :, 1] * o_sel + gates[:, :, 2] * o_win
    return o.reshape(B, T, hd).astype(h.dtype) @ w_o


def _forward(x, c, conv_hist, past_kv, past_win, win_keep, p):
    B, T, D = x.shape
    pos0 = 0 if past_kv is None else past_kv.shape[1]
    qpos = pos0 + jnp.arange(T, dtype=jnp.int32)
    sc = jax.nn.silu(c)
    new_hist = []
    shared = None
    kv_rows = None
    win_state = None
    for l in range(DEPTH):
        mod = (sc @ p['ada_w'][l] + p['ada_b'][l]).reshape(B, 3, 3, D)
        if l == N_A_LAYERS:
            kvm = (sc @ p['kv_ada_w'] + p['kv_ada_b']).reshape(B, 2, D)
            hk = _modulate(x, p['kv_norm_g'], kvm[:, 0], kvm[:, 1])
            kv = (hk @ p['w_kv']).reshape(B, T, 6, N_KV, HEAD_DIM)
            k_sel = _rms(kv[:, :, 2], p['k_norm_g'][1])
            k_win = _rms(kv[:, :, 4], p['k_norm_g'][2])
            kv_rows = jnp.stack([kv[:, :, 0], kv[:, :, 1], k_sel, kv[:, :, 3]], axis=2)
            win_rows = jnp.stack([k_win, kv[:, :, 5]], axis=2)
            full_kv = kv_rows if past_kv is None else jnp.concatenate([past_kv, kv_rows], axis=1)
            win_all = win_rows if past_win is None else jnp.concatenate([past_win, win_rows], axis=1)
            win_state = win_all[:, -win_keep:]
            kc, vc = _compress(full_kv[:, :, 0:2], p['cmp_pos'], p['cmp_w1'], p['cmp_b1'], p['cmp_w2'], p['k_norm_g'][0])
            shared = (kc, vc, _selection_blocks(full_kv[:, :, 2:4]), win_all, past_win is None)
        h = _modulate(x, p['norm_g'][l, 0], mod[:, 0, 0], mod[:, 0, 1])
        x = x + HALF * mod[:, 0, 2][:, None] * _swiglu(h, p['ffn_w_in'][l, 0], p['ffn_w_out'][l, 0])
        h = _modulate(x, p['norm_g'][l, 1], mod[:, 1, 0], mod[:, 1, 1])
        if l < N_A_LAYERS:
            out, hist = _conv_module(h, conv_hist[l], p['conv_w_in'][l], p['conv_b_in'][l], p['conv_dw'][l],
                                     p['conv_dw_b'][l], p['conv_ln_g'][l], p['conv_ln_b'][l],
                                     p['conv_w_out'][l], p['conv_b_out'][l])
            new_hist.append(hist)
        else:
            j = l - N_A_LAYERS
            out = _nsa(h, qpos, shared, p['w_qg'][j], p['q_norm_g'][j], p['w_o'][j])
        x = x + mod[:, 1, 2][:, None] * out
        h = _modulate(x, p['norm_g'][l, 2], mod[:, 2, 0], mod[:, 2, 1])
        x = x + HALF * mod[:, 2, 2][:, None] * _swiglu(h, p['ffn_w_in'][l, 1], p['ffn_w_out'][l, 1])
    return x, kv_rows, win_state, jnp.stack(new_hist)


def setup_inputs(seed: int = 0) -> dict:
    key = jax.random.key(seed)
    keys = jax.random.split(key, 40)
    cnt = [0]

    def nk():
        k = keys[cnt[0]]
        cnt[0] += 1
        return k

    def nrm(shape, s=1.0):
        return jax.random.normal(nk(), shape, jnp.float32) * s

    d = D_MODEL
    hd = N_HEADS * HEAD_DIM
    n_pages = PAST_LEN // PAGE_SIZE
    n_pool = (5 * DEC_BATCH * n_pages + 3) // 4
    w_buf = min(WINDOW, PAST_LEN)
    inp = {}
    inp['x_prompt'] = nrm((BATCH, SEQ, d))
    inp['x_sample'] = nrm((DEC_BATCH, DEC_SEQ, d))
    inp['c_prompt'] = nrm((BATCH, d))
    inp['c_sample'] = nrm((DEC_BATCH, d))
    inp['cache_kv'] = nrm((n_pool, PAGE_SIZE, 4, N_KV, HEAD_DIM))
    inp['cache_win'] = nrm((DEC_BATCH, w_buf, 2, N_KV, HEAD_DIM))
    inp['state_conv'] = nrm((N_A_LAYERS, DEC_BATCH, CONV_W - 1, d), 0.5)
    inp['page_table'] = jax.random.permutation(nk(), n_pool)[: DEC_BATCH * n_pages].reshape(DEC_BATCH, n_pages).astype(jnp.int32)
    inp['ada_w'] = nrm((DEPTH, d, 9 * d), 0.5 * d ** -0.5)
    inp['ada_b'] = nrm((DEPTH, 9 * d), 0.02)
    inp['norm_g'] = 1.0 + nrm((DEPTH, 3, d), 0.05)
    inp['ffn_w_in'] = nrm((DEPTH, 2, d, 2 * D_FF), d ** -0.5)
    inp['ffn_w_out'] = nrm((DEPTH, 2, D_FF, d), D_FF ** -0.5)
    inp['conv_w_in'] = nrm((N_A_LAYERS, d, 2 * d), d ** -0.5)
    inp['conv_b_in'] = nrm((N_A_LAYERS, 2 * d), 0.02)
    inp['conv_dw'] = nrm((N_A_LAYERS, CONV_W, d), CONV_W ** -0.5)
    inp['conv_dw_b'] = nrm((N_A_LAYERS, d), 0.02)
    inp['conv_ln_g'] = 1.0 + nrm((N_A_LAYERS, d), 0.05)
    inp['conv_ln_b'] = nrm((N_A_LAYERS, d), 0.02)
    inp['conv_w_out'] = nrm((N_A_LAYERS, d, d), d ** -0.5)
    inp['conv_b_out'] = nrm((N_A_LAYERS, d), 0.02)
    inp['kv_norm_g'] = 1.0 + nrm((d,), 0.05)
    inp['kv_ada_w'] = nrm((d, 2 * d), 0.5 * d ** -0.5)
    inp['kv_ada_b'] = nrm((2 * d,), 0.02)
    inp['w_kv'] = nrm((d, 6 * N_KV * HEAD_DIM), d ** -0.5)
    inp['cmp_pos'] = nrm((BLK, 2, HEAD_DIM), 0.1)
    inp['cmp_w1'] = nrm((2, BLK * HEAD_DIM, CMP_HID), (BLK * HEAD_DIM) ** -0.5)
    inp['cmp_b1'] = nrm((2, CMP_HID), 0.02)
    inp['cmp_w2'] = nrm((2, CMP_HID, HEAD_DIM), CMP_HID ** -0.5)
    inp['k_norm_g'] = 1.0 + nrm((3, HEAD_DIM), 0.05)
    inp['w_qg'] = nrm((N_B_LAYERS, d, hd + 3 * N_HEADS), d ** -0.5)
    inp['q_norm_g'] = 1.0 + nrm((N_B_LAYERS, HEAD_DIM), 0.05)
    inp['w_o'] = nrm((N_B_LAYERS, hd, d), hd ** -0.5)
    return inp


def reference(x_prompt, x_sample, c_prompt, c_sample, cache_kv, cache_win, state_conv, page_table,
              ada_w, ada_b, norm_g, ffn_w_in, ffn_w_out,
              conv_w_in, conv_b_in, conv_dw, conv_dw_b, conv_ln_g, conv_ln_b, conv_w_out, conv_b_out,
              kv_norm_g, kv_ada_w, kv_ada_b, w_kv, cmp_pos, cmp_w1, cmp_b1, cmp_w2, k_norm_g,
              w_qg, q_norm_g, w_o):
    params = dict(ada_w=ada_w, ada_b=ada_b, norm_g=norm_g, ffn_w_in=ffn_w_in, ffn_w_out=ffn_w_out,
                  conv_w_in=conv_w_in, conv_b_in=conv_b_in, conv_dw=conv_dw, conv_dw_b=conv_dw_b,
                  conv_ln_g=conv_ln_g, conv_ln_b=conv_ln_b, conv_w_out=conv_w_out, conv_b_out=conv_b_out,
                  kv_norm_g=kv_norm_g, kv_ada_w=kv_ada_w, kv_ada_b=kv_ada_b, w_kv=w_kv,
                  cmp_pos=cmp_pos, cmp_w1=cmp_w1, cmp_b1=cmp_b1, cmp_w2=cmp_w2, k_norm_g=k_norm_g,
                  w_qg=w_qg, q_norm_g=q_norm_g, w_o=w_o)
    bp, tp = x_prompt.shape[0], x_prompt.shape[1]
    hist0 = jnp.zeros((N_A_LAYERS, bp, CONV_W - 1, D_MODEL), x_prompt.dtype)
    y_prompt, kv_prompt, win_prompt, conv_prompt = _forward(
        x_prompt, c_prompt, hist0, None, None, min(WINDOW, tp), params)
    bd, n_pages = page_table.shape
    past_kv = cache_kv[page_table].reshape(bd, n_pages * cache_kv.shape[1], 4, N_KV, HEAD_DIM)
    y_sample, kv_sample, win_sample, conv_sample = _forward(
        x_sample, c_sample, state_conv, past_kv, cache_win, cache_win.shape[1], params)
    return (y_prompt, y_sample, kv_prompt, kv_sample, win_prompt, win_sample, conv_prompt, conv_sample)
```

```python
import functools

import numpy as np
import jax
import jax.numpy as jnp
from jax import lax
from jax.experimental import pallas as pl
from jax.experimental.pallas import tpu as pltpu

HEAD_DIM = 128
N_KV = 4
GROUP = 4
BLK = 64
TOP_N = 16
WINDOW = 512
CONV_W = 31
HALO = 32
EPS = 1e-6
NEG = -1e30
FORCE = 1e4
HALF = 0.5
SCALE = HEAD_DIM ** -0.5
KV_SLOT = N_KV * HEAD_DIM

BF = jnp.bfloat16
F32 = jnp.float32

V7X_VMEM_BYTES = 64 * 1024 * 1024
VMEM_LIMIT = V7X_VMEM_BYTES - 8 * 1024 * 1024


def _params(*sem):
    return pltpu.CompilerParams(dimension_semantics=sem, vmem_limit_bytes=VMEM_LIMIT)


def _dot(a, b):
    return jnp.dot(a, b, preferred_element_type=F32)


def _dot_nt(a, b):
    return lax.dot_general(a, b, (((1,), (1,)), ((), ())), preferred_element_type=F32)


def _sigmoid(x):
    return jax.nn.sigmoid(x)


def _rms_groups(y, gain):
    outs = []
    for c in range(y.shape[-1] // HEAD_DIM):
        ch = y[:, c * HEAD_DIM:(c + 1) * HEAD_DIM]
        outs.append(ch * lax.rsqrt(jnp.mean(ch * ch, axis=-1, keepdims=True) + EPS))
    return jnp.concatenate(outs, axis=-1) * gain


def _modulate_into(h_ref, x_ref, g_ref, sh_ref, sc_ref):
    x = x_ref[...]
    bb, tt, d = x.shape
    y = x * lax.rsqrt(jnp.mean(x * x, axis=-1, keepdims=True) + EPS) * g_ref[...]
    h = y * (1.0 + sc_ref[...]) + sh_ref[...]
    h_ref[...] = h.reshape(bb * tt, d).astype(BF)


def _ada_kernel(c_ref, w_ref, b_ref, o_ref):
    c = c_ref[...]
    sc = (c * _sigmoid(c)).astype(BF)
    o_ref[...] = _dot(sc, w_ref[...].astype(BF)) + b_ref[...]


def _ada(c, w, b, tn):
    nl, k, n = w.shape
    m = c.shape[0]
    return pl.pallas_call(
        _ada_kernel,
        out_shape=jax.ShapeDtypeStruct((nl, m, n), F32),
        grid=(nl, n // tn),
        in_specs=[pl.BlockSpec((m, k), lambda l, j: (0, 0)),
                  pl.BlockSpec((None, k, tn), lambda l, j: (l, 0, j)),
                  pl.BlockSpec((None, 1, tn), lambda l, j: (l, 0, j))],
        out_specs=pl.BlockSpec((None, m, tn), lambda l, j: (l, 0, j)),
        compiler_params=_params("parallel", "parallel"),
        name="ada_proj",
    )(c, w, b.reshape(nl, 1, n))


def _x_spec(bb, tt, d):
    return pl.BlockSpec((bb, tt, d), lambda b, i, j: (b, i, 0))


def _mod_spec(bb, d, l, idx):
    return pl.BlockSpec((None, bb, None, 1, d), lambda b, i, j: (l, b, idx, 0, 0))


def _gain_spec(d, l, k):
    return pl.BlockSpec((None, None, 1, d), lambda b, i, j: (l, k, 0, 0))


def _ffn_kernel(x_ref, sh_ref, sc_ref, gt_ref, g_ref, wa_ref, wb_ref, wo_ref, o_ref, h_ref, acc_ref):
    j = pl.program_id(2)
    bb, tt, d = x_ref.shape

    @pl.when(j == 0)
    def _():
        _modulate_into(h_ref, x_ref, g_ref, sh_ref, sc_ref)

    h = h_ref[...]
    a = _dot(h, wa_ref[...].astype(BF))
    b = _dot(h, wb_ref[...].astype(BF))
    act = (a * _sigmoid(a) * b).astype(BF)
    part = _dot(act, wo_ref[...].astype(BF))

    @pl.when(j == 0)
    def _():
        acc_ref[...] = part

    @pl.when(j > 0)
    def _():
        acc_ref[...] += part

    @pl.when(j == pl.num_programs(2) - 1)
    def _():
        o_ref[...] = x_ref[...] + HALF * gt_ref[...] * acc_ref[...].reshape(bb, tt, d)


def _ffn(x, mod5, norm4, w_in, w_out, l, sub, mod_base, norm_idx, bb, tt, tf):
    bsz, t, d = x.shape
    f = w_out.shape[2]
    nf = f // tf
    return pl.pallas_call(
        _ffn_kernel,
        out_shape=jax.ShapeDtypeStruct(x.shape, F32),
        grid=(bsz // bb, t // tt, nf),
        in_specs=[_x_spec(bb, tt, d),
                  _mod_spec(bb, d, l, mod_base), _mod_spec(bb, d, l, mod_base + 1),
                  _mod_spec(bb, d, l, mod_base + 2),
                  _gain_spec(d, l, norm_idx),
                  pl.BlockSpec((None, None, d, tf), lambda b, i, j: (l, sub, 0, j)),
                  pl.BlockSpec((None, None, d, tf), lambda b, i, j: (l, sub, 0, nf + j)),
                  pl.BlockSpec((None, None, tf, d), lambda b, i, j: (l, sub, j, 0))],
        out_specs=_x_spec(bb, tt, d),
        scratch_shapes=[pltpu.VMEM((bb * tt, d), BF), pltpu.VMEM((bb * tt, d), F32)],
        compiler_params=_params("parallel", "parallel", "arbitrary"),
        name="ffn",
    )(x, mod5, mod5, mod5, norm4, w_in, w_in, w_out)


def _glu_kernel(x_ref, sh_ref, sc_ref, g_ref, wa_ref, wg_ref, ba_ref, bg_ref, o_ref, h_ref):
    j = pl.program_id(2)
    bb, tt, _ = x_ref.shape

    @pl.when(j == 0)
    def _():
        _modulate_into(h_ref, x_ref, g_ref, sh_ref, sc_ref)

    h = h_ref[...]
    a = _dot(h, wa_ref[...].astype(BF)) + ba_ref[...]
    g = _dot(h, wg_ref[...].astype(BF)) + bg_ref[...]
    o_ref[...] = (a * _sigmoid(g)).reshape(bb, tt, a.shape[-1])


def _glu(x, mod5, norm4, w_in, b_in, l, bb, tt, tn):
    bsz, t, d = x.shape
    nn = d // tn
    b3 = b_in.reshape(b_in.shape[0], 1, 2 * d)
    return pl.pallas_call(
        _glu_kernel,
        out_shape=jax.ShapeDtypeStruct(x.shape, F32),
        grid=(bsz // bb, t // tt, nn),
        in_specs=[_x_spec(bb, tt, d),
                  _mod_spec(bb, d, l, 3), _mod_spec(bb, d, l, 4),
                  _gain_spec(d, l, 1),
                  pl.BlockSpec((None, d, tn), lambda b, i, j: (l, 0, j)),
                  pl.BlockSpec((None, d, tn), lambda b, i, j: (l, 0, nn + j)),
                  pl.BlockSpec((None, 1, tn), lambda b, i, j: (l, 0, j)),
                  pl.BlockSpec((None, 1, tn), lambda b, i, j: (l, 0, nn + j))],
        out_specs=pl.BlockSpec((bb, tt, tn), lambda b, i, j: (b, i, j)),
        scratch_shapes=[pltpu.VMEM((bb * tt, d), BF)],
        compiler_params=_params("parallel", "parallel", "arbitrary"),
        name="conv_glu",
    )(x, mod5, mod5, norm4, w_in, w_in, b3, b3)


CONV_ROWS = 8
CONV_LANES = 512


def _conv_tail_kernel(u_ref, halo_ref, hist_ref, x_ref, gt_ref, dw_ref, dwb_ref, lng_ref, lnb_ref,
                      wo_ref, bo_ref, o_ref, ext_ref, z_ref):
    i = pl.program_id(1)
    j = pl.program_id(2)
    bb, tt, d = u_ref.shape

    @pl.when(j == 0)
    def _():
        ext_ref[:, 0:HALO, :] = jnp.where(i == 0, hist_ref[...], halo_ref[...])
        ext_ref[:, HALO:HALO + tt, :] = u_ref[...]
        off = HALO - (CONV_W - 1)

        def rows(c, carry):
            r0 = pl.multiple_of(c * CONV_ROWS, CONV_ROWS)
            parts = []
            for c0 in range(0, d, CONV_LANES):
                lanes = slice(c0, c0 + CONV_LANES)
                win = ext_ref[:, pl.ds(r0, CONV_ROWS + HALO), lanes]
                part = jnp.zeros((bb, CONV_ROWS, CONV_LANES), F32) + dwb_ref[:, lanes]
                for w in range(CONV_W):
                    part = part + dw_ref[w:w + 1, lanes] * win[:, off + w:off + w + CONV_ROWS, :]
                parts.append(part)
            acc = jnp.concatenate(parts, axis=-1)
            mu = jnp.mean(acc, axis=-1, keepdims=True)
            cen = acc - mu
            var = jnp.mean(cen * cen, axis=-1, keepdims=True)
            y = cen * lax.rsqrt(var + EPS) * lng_ref[...] + lnb_ref[...]
            z = y * _sigmoid(y)
            for bi in range(bb):
                z_ref[pl.ds(bi * tt + r0, CONV_ROWS), :] = z[bi]
            return carry

        lax.fori_loop(0, tt // CONV_ROWS, rows, 0)

    out = _dot(z_ref[...].astype(BF), wo_ref[...].astype(BF)) + bo_ref[...]
    o_ref[...] = x_ref[...] + gt_ref[...] * out.reshape(bb, tt, out.shape[-1])


def _conv_tail(u, hist_pad, x, mod5, dw, dw_b, ln_g, ln_b, w_out, b_out, l, bb, tt, tn):
    bsz, t, d = x.shape
    nl = dw.shape[0]
    if t > tt:
        per = tt // HALO
        halo_arr = u
        halo_spec = pl.BlockSpec((bb, HALO, d), lambda b, i, j: (b, jnp.maximum(i * per - 1, 0), 0))
    else:
        halo_arr = hist_pad
        halo_spec = pl.BlockSpec((bb, HALO, d), lambda b, i, j: (b, 0, 0))
    vec = lambda a: a.reshape(nl, 1, d)
    row_spec = pl.BlockSpec((None, 1, d), lambda b, i, j: (l, 0, 0))
    return pl.pallas_call(
        _conv_tail_kernel,
        out_shape=jax.ShapeDtypeStruct(x.shape, F32),
        grid=(bsz // bb, t // tt, d // tn),
        in_specs=[_x_spec(bb, tt, d),
                  halo_spec,
                  pl.BlockSpec((bb, HALO, d), lambda b, i, j: (b, 0, 0)),
                  pl.BlockSpec((bb, tt, tn), lambda b, i, j: (b, i, j)),
                  pl.BlockSpec((None, bb, None, 1, tn), lambda b, i, j: (l, b, 5, 0, j)),
                  pl.BlockSpec((None, CONV_W, d), lambda b, i, j: (l, 0, 0)),
                  row_spec, row_spec, row_spec,
                  pl.BlockSpec((None, d, tn), lambda b, i, j: (l, 0, j)),
                  pl.BlockSpec((None, 1, tn), lambda b, i, j: (l, 0, j))],
        out_specs=pl.BlockSpec((bb, tt, tn), lambda b, i, j: (b, i, j)),
        scratch_shapes=[pltpu.VMEM((bb, HALO + tt, d), F32), pltpu.VMEM((bb * tt, d), F32)],
        compiler_params=_params("parallel", "parallel", "arbitrary"),
        name="conv_tail",
    )(u, halo_arr, hist_pad, x, mod5, dw, vec(dw_b), vec(ln_g), vec(ln_b), w_out, vec(b_out))


def _kv_kernel(x_ref, sh_ref, sc_ref, g_ref, w_ref, kg_ref, o_ref, h_ref):
    j = pl.program_id(2)
    bb, tt, _ = x_ref.shape

    @pl.when(j == 0)
    def _():
        _modulate_into(h_ref, x_ref, g_ref, sh_ref, sc_ref)

    y = _dot(h_ref[...], w_ref[...].astype(BF))
    normed = _rms_groups(y, kg_ref[...])
    y = jnp.where((j == 2) | (j == 4), normed, y)
    o_ref[...] = y.reshape(bb, tt, y.shape[-1])


def _kv_proj(x, kvmod5, kv_norm_g, w_kv, k_norm_g, bb, tt):
    bsz, t, d = x.shape
    n = w_kv.shape[1]
    ns = n // KV_SLOT
    gains = jnp.ones((ns, 1, KV_SLOT), F32)
    gains = gains.at[2, 0].set(jnp.tile(k_norm_g[1], N_KV)).at[4, 0].set(jnp.tile(k_norm_g[2], N_KV))
    return pl.pallas_call(
        _kv_kernel,
        out_shape=jax.ShapeDtypeStruct((bsz, t, n), F32),
        grid=(bsz // bb, t // tt, ns),
        in_specs=[_x_spec(bb, tt, d),
                  _mod_spec(bb, d, 0, 0), _mod_spec(bb, d, 0, 1),
                  pl.BlockSpec((1, d), lambda b, i, j: (0, 0)),
                  pl.BlockSpec((d, KV_SLOT), lambda b, i, j: (0, j)),
                  pl.BlockSpec((None, 1, KV_SLOT), lambda b, i, j: (j, 0, 0))],
        out_specs=pl.BlockSpec((bb, tt, KV_SLOT), lambda b, i, j: (b, i, j)),
        scratch_shapes=[pltpu.VMEM((bb * tt, d), BF)],
        compiler_params=_params("parallel", "parallel", "arbitrary"),
        name="kv_proj",
    )(x, kvmod5, kvmod5, kv_norm_g.reshape(1, d), w_kv, gains)


def _q_kernel(x_ref, sh_ref, sc_ref, g_ref, wq_ref, wg_ref, qg_ref, q_ref, gate_ref, h_ref):
    j = pl.program_id(2)
    bb, tt, _ = x_ref.shape

    @pl.when(j == 0)
    def _():
        _modulate_into(h_ref, x_ref, g_ref, sh_ref, sc_ref)
        gl = _dot(h_ref[...], wg_ref[...].astype(BF))
        gate_ref[...] = _sigmoid(gl).reshape(bb, tt, gl.shape[-1])

    y = _dot(h_ref[...], wq_ref[...].astype(BF))
    q_ref[...] = _rms_groups(y, qg_ref[...]).reshape(bb, tt, y.shape[-1])


def _q_proj(x, mod5, norm4, w_q, w_gate, q_gain, l, lj, bb, tt):
    bsz, t, d = x.shape
    hd = w_q.shape[2]
    ng = w_gate.shape[2]
    return pl.pallas_call(
        _q_kernel,
        out_shape=(jax.ShapeDtypeStruct((bsz, t, hd), F32), jax.ShapeDtypeStruct((bsz, t, ng), F32)),
        grid=(bsz // bb, t // tt, hd // KV_SLOT),
        in_specs=[_x_spec(bb, tt, d),
                  _mod_spec(bb, d, l, 3), _mod_spec(bb, d, l, 4),
                  _gain_spec(d, l, 1),
                  pl.BlockSpec((None, d, KV_SLOT), lambda b, i, j: (lj, 0, j)),
                  pl.BlockSpec((None, d, ng), lambda b, i, j: (lj, 0, 0)),
                  pl.BlockSpec((None, 1, KV_SLOT), lambda b, i, j: (lj, 0, 0))],
        out_specs=(pl.BlockSpec((bb, tt, KV_SLOT), lambda b, i, j: (b, i, j)),
                   pl.BlockSpec((bb, tt, ng), lambda b, i, j: (b, i, 0))),
        scratch_shapes=[pltpu.VMEM((bb * tt, d), BF)],
        compiler_params=_params("parallel", "parallel", "arbitrary"),
        name="q_proj",
    )(x, mod5, mod5, norm4, w_q, w_gate, q_gain)


def _oproj_kernel(a_ref, x_ref, gt_ref, w_ref, o_ref):
    bb, tt, k = a_ref.shape
    out = _dot(a_ref[...].reshape(bb * tt, k).astype(BF), w_ref[...].astype(BF))
    o_ref[...] = x_ref[...] + gt_ref[...] * out.reshape(bb, tt, out.shape[-1])


def _o_proj(a, x, mod5, w_o, l, lj, bb, tt, tn):
    bsz, t, d = x.shape
    k = a.shape[2]
    return pl.pallas_call(
        _oproj_kernel,
        out_shape=jax.ShapeDtypeStruct(x.shape, F32),
        grid=(bsz // bb, t // tt, d // tn),
        in_specs=[_x_spec(bb, tt, k),
                  pl.BlockSpec((bb, tt, tn), lambda b, i, j: (b, i, j)),
                  pl.BlockSpec((None, bb, None, 1, tn), lambda b, i, j: (l, b, 5, 0, j)),
                  pl.BlockSpec((None, k, tn), lambda b, i, j: (lj, 0, j))],
        out_specs=pl.BlockSpec((bb, tt, tn), lambda b, i, j: (b, i, j)),
        compiler_params=_params("parallel", "parallel", "arbitrary"),
        name="o_proj",
    )(a, x, mod5, w_o)


def _compress_body(head_rows, nb, pos_ref, w1_ref, b1_ref, w2_ref, kg_ref, is_key, o_ref):
    acc = jnp.zeros((N_KV * nb, w1_ref.shape[-1]), F32)
    for j in range(BLK):
        pos = pos_ref[j:j + 1, :]
        lhs = jnp.concatenate([rows[pl.ds(j, nb, stride=BLK), :] + pos for rows in head_rows], axis=0)
        acc = acc + _dot(lhs.astype(BF), w1_ref[j * HEAD_DIM:(j + 1) * HEAD_DIM, :].astype(BF))
    hid = jax.nn.gelu(acc + b1_ref[...])
    out = _dot(hid.astype(BF), w2_ref[...].astype(BF))
    normed = out * lax.rsqrt(jnp.mean(out * out, axis=-1, keepdims=True) + EPS) * kg_ref[...]
    out = jnp.where(is_key, normed, out)
    o_ref[...] = out.reshape(N_KV, nb, HEAD_DIM)


def _compress_prompt_kernel(r0_ref, r1_ref, r2_ref, r3_ref, pos_ref, w1_ref, b1_ref, w2_ref, kg_ref, o_ref):
    nb = r0_ref.shape[0] // BLK
    _compress_body([r0_ref, r1_ref, r2_ref, r3_ref], nb, pos_ref, w1_ref, b1_ref, w2_ref, kg_ref,
                   pl.program_id(0) == 0, o_ref)


def _cmp_weight_specs():
    kf = BLK * HEAD_DIM
    return [pl.BlockSpec((None, BLK, HEAD_DIM), lambda p, *_: (p, 0, 0)),
            pl.BlockSpec((None, kf, 2 * HEAD_DIM), lambda p, *_: (p, 0, 0)),
            pl.BlockSpec((None, 1, 2 * HEAD_DIM), lambda p, *_: (p, 0, 0)),
            pl.BlockSpec((None, 2 * HEAD_DIM, HEAD_DIM), lambda p, *_: (p, 0, 0)),
            pl.BlockSpec((1, HEAD_DIM), lambda p, *_: (0, 0))]


def _compress_prompt(kv_all, pos_t, w1, b1, w2, kgain):
    bsz, t, _ = kv_all.shape
    nc = t // BLK

    def head_spec(g):
        return pl.BlockSpec((None, nc * BLK, HEAD_DIM), lambda p, b: (b, 0, p * N_KV + g))

    return pl.pallas_call(
        _compress_prompt_kernel,
        out_shape=jax.ShapeDtypeStruct((2, bsz, N_KV, nc, HEAD_DIM), F32),
        grid=(2, bsz),
        in_specs=[head_spec(g) for g in range(N_KV)] + _cmp_weight_specs(),
        out_specs=pl.BlockSpec((None, None, N_KV, nc, HEAD_DIM), lambda p, b: (p, b, 0, 0, 0)),
        compiler_params=_params("arbitrary", "arbitrary"),
        name="compress_prompt",
    )(kv_all, kv_all, kv_all, kv_all, pos_t, w1, b1.reshape(2, 1, -1), w2, kgain)


def _compress_paged_kernel(pt_ref, cache_ref, pos_ref, w1_ref, b1_ref, w2_ref, kg_ref, o_ref, rows_ref, sem):
    p = pl.program_id(0)
    b = pl.program_id(1)
    s = pl.program_id(2)
    page = cache_ref.shape[1]
    n_pg = rows_ref.shape[1] // page

    def head_copy(k, g):
        col = pl.multiple_of((p * N_KV + g) * HEAD_DIM, HEAD_DIM)
        src = cache_ref.at[pt_ref[b, s * n_pg + k], :, pl.ds(col, HEAD_DIM)]
        return pltpu.make_async_copy(src, rows_ref.at[g, pl.ds(k * page, page), :], sem)

    def start(k, c):
        for g in range(N_KV):
            head_copy(k, g).start()
        return c

    def wait(k, c):
        for g in range(N_KV):
            head_copy(k, g).wait()
        return c

    lax.fori_loop(0, n_pg, start, 0)
    lax.fori_loop(0, n_pg, wait, 0)
    _compress_body([rows_ref.at[g] for g in range(N_KV)], rows_ref.shape[1] // BLK, pos_ref, w1_ref, b1_ref,
                   w2_ref, kg_ref, p == 0, o_ref)


def _compress_paged(cache2, page_table, pos_t, w1, b1, w2, kgain, pages_per_step):
    bsz, n_pages = page_table.shape
    page = cache2.shape[1]
    nb = pages_per_step * page // BLK
    nc = n_pages * page // BLK
    grid_spec = pltpu.PrefetchScalarGridSpec(
        num_scalar_prefetch=1,
        grid=(2, bsz, n_pages // pages_per_step),
        in_specs=[pl.BlockSpec(memory_space=pl.ANY)] + _cmp_weight_specs(),
        out_specs=pl.BlockSpec((None, None, N_KV, nb, HEAD_DIM), lambda p, b, s, pt: (p, b, 0, s, 0)),
        scratch_shapes=[pltpu.VMEM((N_KV, pages_per_step * page, HEAD_DIM), F32), pltpu.SemaphoreType.DMA],
    )
    return pl.pallas_call(
        _compress_paged_kernel,
        out_shape=jax.ShapeDtypeStruct((2, bsz, N_KV, nc, HEAD_DIM), F32),
        grid_spec=grid_spec,
        compiler_params=_params("arbitrary", "arbitrary", "arbitrary"),
        name="compress_paged",
    )(page_table, cache2, pos_t, w1, b1.reshape(2, 1, -1), w2, kgain)


def _stack_heads(q_ref_val):
    return jnp.concatenate([q_ref_val[:, r * HEAD_DIM:(r + 1) * HEAD_DIM] for r in range(GROUP)], axis=0)


M_FLOOR = -1e29


def _masked_scores(q, k_bf, valid, dist, slopes_ref, g):
    t = dist.shape[0]
    raw = _dot_nt(q, k_bf)
    return jnp.concatenate(
        [jnp.where(valid, raw[r * t:(r + 1) * t] * SCALE - slopes_ref[g, r] * dist, NEG) for r in range(GROUP)],
        axis=0)


def _softmax_full(s):
    m = jnp.maximum(jnp.max(s, axis=-1, keepdims=True), M_FLOOR)
    e = jnp.exp(s - m)
    return e, jnp.sum(e, axis=-1, keepdims=True)


def _online_update(s, v_bf, m_old, l_old, acc_old):
    m_new = jnp.maximum(m_old, jnp.max(s, axis=-1, keepdims=True))
    e = jnp.exp(s - jnp.maximum(m_new, M_FLOOR))
    alpha = jnp.exp(m_old - m_new)
    l_new = alpha * l_old + jnp.sum(e, axis=-1, keepdims=True)
    acc_new = alpha * acc_old + _dot(e.astype(BF), v_bf)
    return m_new, l_new, acc_new


def _head_sum(p, t):
    out = p[0:t]
    for r in range(1, GROUP):
        out = out + p[r * t:(r + 1) * t]
    return out


def _top_blocks(score, blk, k):
    n = score.shape[-1]
    sel = jnp.zeros(score.shape, F32)
    for _ in range(k):
        mx = jnp.max(score, axis=-1, keepdims=True)
        idx = jnp.min(jnp.where(score == mx, blk, float(n)), axis=-1, keepdims=True)
        hit = blk == idx
        sel = jnp.where(hit & (mx >= 0.0), 1.0, sel)
        score = jnp.where(hit, -2.0, score)
    return sel


def _expand_blocks(sel, first_blk, n_keys):
    nsel = sel.shape[-1]
    row = lax.broadcasted_iota(jnp.int32, (nsel, n_keys), 0)
    col = lax.broadcasted_iota(jnp.int32, (nsel, n_keys), 1)
    expand = jnp.where(row == first_blk + col // BLK, 1.0, 0.0).astype(BF)
    return _dot(sel.astype(BF), expand)


def _combine(gate, o_cmp, o_sel, o_win, t):
    cols = []
    for r in range(GROUP):
        rs = slice(r * t, (r + 1) * t)
        cols.append(gate[:, r:r + 1] * o_cmp[rs] + gate[:, GROUP + r:GROUP + r + 1] * o_sel[rs]
                    + gate[:, 2 * GROUP + r:2 * GROUP + r + 1] * o_win[rs])
    return jnp.concatenate(cols, axis=-1)


def _nsa_prompt_kernel(slopes_ref, q_ref, gate_ref, kc_ref, vc_ref, ks_ref, vs_ref, kw_ref, vw_ref,
                       o_ref, m_ref, l_ref, acc_ref, *, tk):
    g = pl.program_id(1)
    qi = pl.program_id(2)
    tq = q_ref.shape[0]
    nc = kc_ref.shape[0]
    t0 = qi * tq
    q = _stack_heads(q_ref[...]).astype(BF)

    def tok_pos(n):
        return t0 + lax.broadcasted_iota(jnp.int32, (tq, n), 0)

    blk_i = lax.broadcasted_iota(jnp.int32, (tq, nc), 1)
    dist = (tok_pos(nc) - (blk_i * BLK + (BLK - 1))).astype(F32)
    e, l = _softmax_full(_masked_scores(q, kc_ref[...].astype(BF), dist >= 0, dist, slopes_ref, g))
    p_cmp = e / jnp.maximum(l, 1e-30)
    o_cmp = _dot(p_cmp.astype(BF), vc_ref[...].astype(BF))

    cur = tok_pos(nc) // BLK
    forced = (blk_i == 0) | (blk_i == cur) | (blk_i == cur - 1)
    score = jnp.where(forced, FORCE, _head_sum(p_cmp, tq))
    score = jnp.where(blk_i <= cur, score, -1.0)
    sel = _top_blocks(score, blk_i.astype(F32), min(TOP_N, nc))

    m_ref[...] = jnp.full(m_ref.shape, NEG, F32)
    l_ref[...] = jnp.zeros(l_ref.shape, F32)
    acc_ref[...] = jnp.zeros(acc_ref.shape, F32)

    def key_tile(kt, carry):
        k0 = pl.multiple_of(kt * tk, tk)
        kpos = k0 + lax.broadcasted_iota(jnp.int32, (tq, tk), 1)
        dist = (tok_pos(tk) - kpos).astype(F32)
        valid = (_expand_blocks(sel, kt * (tk // BLK), tk) > 0.5) & (dist >= 0)
        s = _masked_scores(q, ks_ref[pl.ds(k0, tk), :].astype(BF), valid, dist, slopes_ref, g)
        m, l, acc = _online_update(s, vs_ref[pl.ds(k0, tk), :].astype(BF), m_ref[...], l_ref[...], acc_ref[...])
        m_ref[...] = m
        l_ref[...] = l
        acc_ref[...] = acc
        return carry

    lax.fori_loop(0, (t0 + tq + tk - 1) // tk, key_tile, 0)
    o_sel = acc_ref[...] / jnp.maximum(l_ref[...], 1e-30)

    lw = WINDOW + tq
    w0 = pl.multiple_of(jnp.maximum(t0 - WINDOW, 0), 8)
    kpos = w0 + lax.broadcasted_iota(jnp.int32, (tq, lw), 1)
    dist = (tok_pos(lw) - kpos).astype(F32)
    valid = (dist >= 0) & (dist < WINDOW)
    e, l = _softmax_full(_masked_scores(q, kw_ref[pl.ds(w0, lw), :].astype(BF), valid, dist, slopes_ref, g))
    o_win = _dot(e.astype(BF), vw_ref[pl.ds(w0, lw), :].astype(BF)) / jnp.maximum(l, 1e-30)

    o_ref[...] = _combine(gate_ref[...], o_cmp, o_sel, o_win, tq)


def _nsa_prompt(q, gates, cmp_kv, kv_all, slopes, tq, tk):
    bsz, t, hd = q.shape
    nc = cmp_kv.shape[3]
    assert t % BLK == 0 and t % tk == 0 and t >= WINDOW + tq
    rows = GROUP * tq

    def col(slot):
        return pl.BlockSpec((None, t, HEAD_DIM), lambda b, g, i: (b, 0, slot * N_KV + g))

    def cmp_spec(p):
        return pl.BlockSpec((None, None, None, nc, HEAD_DIM), lambda b, g, i: (p, b, g, 0, 0))

    return pl.pallas_call(
        functools.partial(_nsa_prompt_kernel, tk=tk),
        out_shape=jax.ShapeDtypeStruct(q.shape, F32),
        grid=(bsz, N_KV, t // tq),
        in_specs=[pl.BlockSpec(memory_space=pltpu.SMEM),
                  pl.BlockSpec((None, tq, GROUP * HEAD_DIM), lambda b, g, i: (b, i, g)),
                  pl.BlockSpec((None, tq, HEAD_DIM), lambda b, g, i: (b, i, g)),
                  cmp_spec(0), cmp_spec(1), col(2), col(3), col(4), col(5)],
        out_specs=pl.BlockSpec((None, tq, GROUP * HEAD_DIM), lambda b, g, i: (b, i, g)),
        scratch_shapes=[pltpu.VMEM((rows, 1), F32), pltpu.VMEM((rows, 1), F32),
                        pltpu.VMEM((rows, HEAD_DIM), F32)],
        compiler_params=_params("parallel", "parallel", "arbitrary"),
        name="nsa_prompt",
    )(slopes, q, gates, cmp_kv, cmp_kv, kv_all, kv_all, kv_all, kv_all)


def _nsa_paged_kernel(pt_ref, slopes_ref, q_ref, gate_ref, kc_ref, vc_ref, new_ref, cwin_ref, *rest,
                      n_pg, past_len):
    page_refs = rest[:n_pg]
    o_ref, sel_ref, m_ref, l_ref, acc_ref, ocmp_ref, owin_ref, pad_ref = rest[n_pg:]
    s_idx = pl.program_id(1)
    tq = q_ref.shape[0]
    nc = kc_ref.shape[1]
    page = page_refs[0].shape[0]
    rows = GROUP * tq
    npad = pad_ref.shape[0]
    wlen = cwin_ref.shape[0]

    def q_rows(g):
        return _stack_heads(q_ref[:, g * KV_SLOT:(g + 1) * KV_SLOT]).astype(BF)

    def tok_pos(n):
        return past_len + lax.broadcasted_iota(jnp.int32, (tq, n), 0)

    @pl.when(s_idx == 0)
    def _():
        pad_ref[...] = jnp.zeros(pad_ref.shape, F32)
        pad_ref[0:tq, :] = new_ref[...]
        new_i = lax.broadcasted_iota(jnp.int32, (tq, npad), 1)
        for g in range(N_KV):
            q = q_rows(g)
            lanes = slice(g * HEAD_DIM, (g + 1) * HEAD_DIM)
            blk_i = lax.broadcasted_iota(jnp.int32, (tq, nc), 1)
            dist = (tok_pos(nc) - (blk_i * BLK + (BLK - 1))).astype(F32)
            e, l = _softmax_full(_masked_scores(q, kc_ref[g].astype(BF), dist >= 0, dist, slopes_ref, g))
            p_cmp = e / jnp.maximum(l, 1e-30)
            ocmp_ref[g] = _dot(p_cmp.astype(BF), vc_ref[g].astype(BF))
            forced = (blk_i == 0) | (blk_i == nc - 1)
            score = jnp.where(forced, FORCE, _head_sum(p_cmp, tq))
            sel_ref[g] = _top_blocks(score, blk_i.astype(F32), min(TOP_N, nc + 1) - 1)
            dist = (tok_pos(npad) - (past_len + new_i)).astype(F32)
            valid = (dist >= 0) & (new_i < tq)
            k_new = pad_ref[:, 2 * KV_SLOT + g * HEAD_DIM:2 * KV_SLOT + (g + 1) * HEAD_DIM].astype(BF)
            v_new = pad_ref[:, 3 * KV_SLOT + g * HEAD_DIM:3 * KV_SLOT + (g + 1) * HEAD_DIM].astype(BF)
            s = _masked_scores(q, k_new, valid, dist, slopes_ref, g)
            m, l, acc = _online_update(s, v_new, jnp.full((rows, 1), NEG, F32),
                                       jnp.zeros((rows, 1), F32), jnp.zeros((rows, HEAD_DIM), F32))
            m_ref[g] = m
            l_ref[g] = l
            acc_ref[g] = acc
            kw = jnp.concatenate([cwin_ref[:, lanes],
                                  pad_ref[:, 4 * KV_SLOT + g * HEAD_DIM:4 * KV_SLOT + (g + 1) * HEAD_DIM]], axis=0)
            vw = jnp.concatenate([cwin_ref[:, KV_SLOT + g * HEAD_DIM:KV_SLOT + (g + 1) * HEAD_DIM],
                                  pad_ref[:, 5 * KV_SLOT + g * HEAD_DIM:5 * KV_SLOT + (g + 1) * HEAD_DIM]], axis=0)
            win_i = lax.broadcasted_iota(jnp.int32, (tq, wlen + npad), 1)
            dist = (tok_pos(wlen + npad) - (past_len - wlen + win_i)).astype(F32)
            valid = (dist >= 0) & (dist < WINDOW) & (win_i < wlen + tq)
            e, l = _softmax_full(_masked_scores(q, kw.astype(BF), valid, dist, slopes_ref, g))
            owin_ref[g] = _dot(e.astype(BF), vw.astype(BF)) / jnp.maximum(l, 1e-30)

    nk = n_pg * page
    k0 = s_idx * nk
    kpos = k0 + lax.broadcasted_iota(jnp.int32, (tq, nk), 1)
    dist = (tok_pos(nk) - kpos).astype(F32)
    for g in range(N_KV):
        q = q_rows(g)
        ks = jnp.concatenate([pr[:, g * HEAD_DIM:(g + 1) * HEAD_DIM] for pr in page_refs], axis=0)
        vs = jnp.concatenate([pr[:, KV_SLOT + g * HEAD_DIM:KV_SLOT + (g + 1) * HEAD_DIM] for pr in page_refs],
                             axis=0)
        valid = (_expand_blocks(sel_ref[g], s_idx * (nk // BLK), nk) > 0.5) & (dist >= 0)
        s = _masked_scores(q, ks.astype(BF), valid, dist, slopes_ref, g)
        m, l, acc = _online_update(s, vs.astype(BF), m_ref[g], l_ref[g], acc_ref[g])
        m_ref[g] = m
        l_ref[g] = l
        acc_ref[g] = acc

    @pl.when(s_idx == pl.num_programs(1) - 1)
    def _():
        outs = []
        for g in range(N_KV):
            o_sel = acc_ref[g] / jnp.maximum(l_ref[g], 1e-30)
            outs.append(_combine(gate_ref[:, g * HEAD_DIM:(g + 1) * HEAD_DIM], ocmp_ref[g], o_sel, owin_ref[g], tq))
        o_ref[...] = jnp.concatenate(outs, axis=-1)


def _nsa_paged(q, gates, cmp_kv, kv_new, cache2, cache_win2, page_table, slopes, n_pg):
    bsz, tq, hd = q.shape
    n_pages = page_table.shape[1]
    page = cache2.shape[1]
    past_len = n_pages * page
    nc = cmp_kv.shape[3]
    wlen = cache_win2.shape[1]
    assert past_len % BLK == 0 and tq <= BLK and nc == past_len // BLK and n_pages % n_pg == 0
    assert wlen == WINDOW and tq % 8 == 0
    rows = GROUP * tq
    npad = 128

    def cmp_spec(p):
        return pl.BlockSpec((None, None, N_KV, nc, HEAD_DIM), lambda b, s, pt: (p, b, 0, 0, 0))

    def page_spec(k):
        return pl.BlockSpec((None, page, 2 * KV_SLOT), lambda b, s, pt: (pt[b, s * n_pg + k], 0, 1))

    grid_spec = pltpu.PrefetchScalarGridSpec(
        num_scalar_prefetch=1,
        grid=(bsz, n_pages // n_pg),
        in_specs=[pl.BlockSpec(memory_space=pltpu.SMEM),
                  pl.BlockSpec((None, tq, hd), lambda b, s, pt: (b, 0, 0)),
                  pl.BlockSpec((None, tq, gates.shape[2]), lambda b, s, pt: (b, 0, 0)),
                  cmp_spec(0), cmp_spec(1),
                  pl.BlockSpec((None, tq, kv_new.shape[2]), lambda b, s, pt: (b, 0, 0)),
                  pl.BlockSpec((None, wlen, 2 * KV_SLOT), lambda b, s, pt: (b, 0, 0))]
                 + [page_spec(k) for k in range(n_pg)],
        out_specs=pl.BlockSpec((None, tq, hd), lambda b, s, pt: (b, 0, 0)),
        scratch_shapes=[pltpu.VMEM((N_KV, tq, nc), F32),
                        pltpu.VMEM((N_KV, rows, 1), F32), pltpu.VMEM((N_KV, rows, 1), F32),
                        pltpu.VMEM((N_KV, rows, HEAD_DIM), F32),
                        pltpu.VMEM((N_KV, rows, HEAD_DIM), F32), pltpu.VMEM((N_KV, rows, HEAD_DIM), F32),
                        pltpu.VMEM((npad, kv_new.shape[2]), F32)],
    )
    return pl.pallas_call(
        functools.partial(_nsa_paged_kernel, n_pg=n_pg, past_len=past_len),
        out_shape=jax.ShapeDtypeStruct(q.shape, F32),
        grid_spec=grid_spec,
        compiler_params=_params("parallel", "arbitrary"),
        name="nsa_paged",
    )(page_table, slopes, q, gates, cmp_kv, cmp_kv, kv_new, cache_win2, *([cache2] * n_pg))


def _alibi_slopes():
    n_heads = N_KV * GROUP
    m = np.exp2(-8.0 * np.arange(1, n_heads + 1) / n_heads)
    return jnp.asarray(m, dtype=F32).reshape(N_KV, GROUP)


def _gate_weights(w_qg, hd):
    nl, d, _ = w_qg.shape
    wg = w_qg[:, :, hd:].reshape(nl, d, 3, N_KV, GROUP).transpose(0, 1, 3, 2, 4).reshape(nl, d, N_KV, 3 * GROUP)
    wg = jnp.pad(wg, ((0, 0), (0, 0), (0, 0), (0, HEAD_DIM - 3 * GROUP)))
    return wg.reshape(nl, d, N_KV * HEAD_DIM)


def _forward(x, mod5, kvmod5, hist, paged, p, tiles):
    bsz, t, d = x.shape
    bb, tt = tiles["bb"], tiles["tt"]
    n_a = p["conv_w_in"].shape[0]
    depth = p["ada_w"].shape[0]
    hd = N_KV * GROUP * HEAD_DIM
    norm4 = p["norm_g"].reshape(depth, 3, 1, d)
    new_hist = []
    kv_all = None
    cmp_kv = None
    for l in range(depth):
        if l == n_a:
            kv_all = _kv_proj(x, kvmod5, p["kv_norm_g"], p["w_kv"], p["k_norm_g"], bb, tt)
            if paged is None:
                cmp_kv = _compress_prompt(kv_all, p["pos_t"], p["cmp_w1"], p["cmp_b1"], p["cmp_w2"], p["kgain0"])
            else:
                cmp_kv = _compress_paged(paged["cache2"], paged["page_table"], p["pos_t"], p["cmp_w1"],
                                         p["cmp_b1"], p["cmp_w2"], p["kgain0"], tiles["cmp_pages"])
        x = _ffn(x, mod5, norm4, p["ffn_w_in"], p["ffn_w_out"], l, 0, 0, 0, bb, tt, tiles["tf"])
        if l < n_a:
            u = _glu(x, mod5, norm4, p["conv_w_in"], p["conv_b_in"], l, bb, tt, tiles["tn"])
            hist_pad = jnp.pad(hist[l], ((0, 0), (HALO - (CONV_W - 1), 0), (0, 0)))
            x = _conv_tail(u, hist_pad, x, mod5, p["conv_dw"], p["conv_dw_b"], p["conv_ln_g"], p["conv_ln_b"],
                           p["conv_w_out"], p["conv_b_out"], l, bb, tt, tiles["tn"])
            new_hist.append(jnp.concatenate([hist[l], u], axis=1)[:, -(CONV_W - 1):])
        else:
            lj = l - n_a
            q, gates = _q_proj(x, mod5, norm4, p["w_q"], p["w_gate"], p["q_gain"], l, lj, bb, tt)
            if paged is None:
                a = _nsa_prompt(q, gates, cmp_kv, kv_all, p["slopes"], tiles["tq"], tiles["tk"])
            else:
                a = _nsa_paged(q, gates, cmp_kv, kv_all, paged["cache2"], paged["cache_win2"],
                               paged["page_table"], p["slopes"], tiles["nsa_pages"])
            x = _o_proj(a, x, mod5, p["w_o"], l, lj, bb, tt, tiles["tn"])
        x = _ffn(x, mod5, norm4, p["ffn_w_in"], p["ffn_w_out"], l, 1, 6, 2, bb, tt, tiles["tf"])
    return x, kv_all, jnp.stack(new_hist)


def kernel(x_prompt, x_sample, c_prompt, c_sample, cache_kv, cache_win, state_conv, page_table, ada_w, ada_b, norm_g, ffn_w_in, ffn_w_out, conv_w_in, conv_b_in, conv_dw, conv_dw_b, conv_ln_g, conv_ln_b, conv_w_out, conv_b_out, kv_norm_g, kv_ada_w, kv_ada_b, w_kv, cmp_pos, cmp_w1, cmp_b1, cmp_w2, k_norm_g, w_qg, q_norm_g, w_o):
    bp, tp, d = x_prompt.shape
    bs, ts, _ = x_sample.shape
    depth = ada_w.shape[0]
    hd = N_KV * GROUP * HEAD_DIM
    n_b = w_qg.shape[0]

    n_c = bp + bs
    c_all = jnp.pad(jnp.concatenate([c_prompt, c_sample], axis=0), ((0, -n_c % 16), (0, 0)))
    mod = _ada(c_all, ada_w, ada_b, 1024).reshape(depth, c_all.shape[0], 9, 1, d)
    kvmod = _ada(c_all, kv_ada_w[None], kv_ada_b[None], 1024).reshape(1, c_all.shape[0], 2, 1, d)

    p = dict(ada_w=ada_w, norm_g=norm_g, ffn_w_in=ffn_w_in, ffn_w_out=ffn_w_out,
             conv_w_in=conv_w_in, conv_b_in=conv_b_in, conv_dw=conv_dw, conv_dw_b=conv_dw_b,
             conv_ln_g=conv_ln_g, conv_ln_b=conv_ln_b, conv_w_out=conv_w_out, conv_b_out=conv_b_out,
             kv_norm_g=kv_norm_g, w_kv=w_kv, k_norm_g=k_norm_g,
             cmp_w1=cmp_w1, cmp_b1=cmp_b1, cmp_w2=cmp_w2,
             pos_t=jnp.transpose(cmp_pos, (1, 0, 2)),
             kgain0=k_norm_g[0].reshape(1, HEAD_DIM),
             w_q=w_qg[:, :, :hd], w_gate=_gate_weights(w_qg, hd),
             q_gain=jnp.tile(q_norm_g, (1, GROUP)).reshape(n_b, 1, GROUP * HEAD_DIM),
             w_o=w_o, slopes=_alibi_slopes())

    hist0 = jnp.zeros((conv_dw.shape[0], bp, CONV_W - 1, d), F32)
    tiles_p = dict(bb=1, tt=512, tf=256, tn=512, tq=128, tk=512)
    y_prompt, kv_p, conv_prompt = _forward(x_prompt, mod[:, :bp], kvmod[:, :bp], hist0, None, p, tiles_p)
    kv_prompt = kv_p[:, :, :4 * KV_SLOT].reshape(bp, tp, 4, N_KV, HEAD_DIM)
    wk = min(WINDOW, tp)
    win_prompt = kv_p[:, tp - wk:, 4 * KV_SLOT:].reshape(bp, wk, 2, N_KV, HEAD_DIM)

    n_pool, page = cache_kv.shape[0], cache_kv.shape[1]
    wlen = cache_win.shape[1]
    paged = dict(cache2=cache_kv.reshape(n_pool, page, 4 * KV_SLOT),
                 cache_win2=cache_win.reshape(bs, wlen, 2 * KV_SLOT),
                 page_table=page_table)
    n_pages = page_table.shape[1]
    tiles_s = dict(bb=bs, tt=ts, tf=256, tn=512, cmp_pages=min(64, n_pages), nsa_pages=min(8, n_pages))
    y_sample, kv_s, conv_sample = _forward(x_sample, mod[:, bp:n_c], kvmod[:, bp:n_c], state_conv, paged, p, tiles_s)
    kv_sample = kv_s[:, :, :4 * KV_SLOT].reshape(bs, ts, 4, N_KV, HEAD_DIM)
    win_new = kv_s[:, :, 4 * KV_SLOT:].reshape(bs, ts, 2, N_KV, HEAD_DIM)
    win_sample = jnp.concatenate([cache_win, win_new], axis=1)[:, -wlen:]
    return (y_prompt, y_sample, kv_prompt, kv_sample, win_prompt, win_sample, conv_prompt, conv_sample)
```

```python
import functools
import math

import numpy as np
import jax
import jax.numpy as jnp
from jax import lax
from jax.experimental import pallas as pl
from jax.experimental.pallas import tpu as pltpu

HEAD_DIM = 128
N_KV = 4
GROUP = 4
BLK = 64
TOP_N = 16
WINDOW = 512
CONV_W = 31
HALO = 32
SUBLANES = 8
EPS = 1e-6
NEG = -1e30
BIG = 1e30
M_FLOOR = -1e29
FORCE = 1e4
HALF = 0.5
SCALE = HEAD_DIM ** -0.5
LOG2E = math.log2(math.e)
KV_SLOT = N_KV * HEAD_DIM

BF = jnp.bfloat16
F32 = jnp.float32

V7X_VMEM_BYTES = 64 * 1024 * 1024
VMEM_LIMIT = V7X_VMEM_BYTES - 8 * 1024 * 1024


def _params(*sem):
    return pltpu.CompilerParams(dimension_semantics=sem, vmem_limit_bytes=VMEM_LIMIT)


def _dot(a, b):
    return jnp.dot(a, b, preferred_element_type=F32)


def _dot_nt(a, b):
    return lax.dot_general(a, b, (((1,), (1,)), ((), ())), preferred_element_type=F32)


def _dot_tn(a, b):
    return lax.dot_general(a, b, (((0,), (0,)), ((), ())), preferred_element_type=F32)


def _sigmoid(x):
    return jax.nn.sigmoid(x)


def _rms_groups(y, gain):
    outs = []
    for c in range(y.shape[-1] // HEAD_DIM):
        ch = y[:, c * HEAD_DIM:(c + 1) * HEAD_DIM]
        outs.append(ch * lax.rsqrt(jnp.mean(ch * ch, axis=-1, keepdims=True) + EPS))
    return jnp.concatenate(outs, axis=-1) * gain


def _modulate_into(h_ref, x_ref, g_ref, sh_ref, sc_ref):
    x = x_ref[...]
    bb, tt, d = x.shape
    y = x * lax.rsqrt(jnp.mean(x * x, axis=-1, keepdims=True) + EPS) * g_ref[...]
    h = y * (1.0 + sc_ref[...]) + sh_ref[...]
    h_ref[...] = h.reshape(bb * tt, d).astype(BF)


def _ada_kernel(c_ref, w_ref, b_ref, o_ref):
    c = c_ref[...]
    sc = (c * _sigmoid(c)).astype(BF)
    o_ref[...] = _dot(sc, w_ref[...].astype(BF)) + b_ref[...]


def _ada(c, w, b, tn):
    nl, k, n = w.shape
    m = c.shape[0]
    return pl.pallas_call(
        _ada_kernel,
        out_shape=jax.ShapeDtypeStruct((nl, m, n), F32),
        grid=(nl, n // tn),
        in_specs=[pl.BlockSpec((m, k), lambda l, j: (0, 0)),
                  pl.BlockSpec((None, k, tn), lambda l, j: (l, 0, j)),
                  pl.BlockSpec((None, 1, tn), lambda l, j: (l, 0, j))],
        out_specs=pl.BlockSpec((None, m, tn), lambda l, j: (l, 0, j)),
        compiler_params=_params("parallel", "parallel"),
        name="ada_proj",
    )(c, w, b.reshape(nl, 1, n))


def _x_spec(bb, tt, d):
    return pl.BlockSpec((bb, tt, d), lambda b, i, j: (b, i, 0))


def _mod_spec(bb, d, l, idx):
    return pl.BlockSpec((None, bb, None, 1, d), lambda b, i, j: (l, b, idx, 0, 0))


def _gain_spec(d, l, k):
    return pl.BlockSpec((None, None, 1, d), lambda b, i, j: (l, k, 0, 0))


def _ffn_up_kernel(x_ref, sh_ref, sc_ref, g_ref, wa_ref, wb_ref, o_ref, h_ref):
    @pl.when(pl.program_id(2) == 0)
    def _():
        _modulate_into(h_ref, x_ref, g_ref, sh_ref, sc_ref)

    h = h_ref[...]
    a = _dot(h, wa_ref[...])
    b = _dot(h, wb_ref[...])
    o_ref[...] = (a * _sigmoid(a) * b).astype(BF)


def _ffn_down_kernel(a_ref, x_ref, gt_ref, w_ref, o_ref):
    bb, tt, n = x_ref.shape
    out = _dot(a_ref[...], w_ref[...])
    o_ref[...] = x_ref[...] + HALF * gt_ref[...] * out.reshape(bb, tt, n)


def _ffn(x, mod5, norm4, w_in, w_out, l, sub, mod_base, norm_idx, bb, tt, tf, tn):
    bsz, t, d = x.shape
    f = w_out.shape[2]
    nf = f // tf
    nt = t // tt
    act = pl.pallas_call(
        _ffn_up_kernel,
        out_shape=jax.ShapeDtypeStruct((bsz * t, f), BF),
        grid=(bsz // bb, nt, nf),
        in_specs=[_x_spec(bb, tt, d),
                  _mod_spec(bb, d, l, mod_base), _mod_spec(bb, d, l, mod_base + 1),
                  _gain_spec(d, l, norm_idx),
                  pl.BlockSpec((None, None, d, tf), lambda b, i, j: (l, sub, 0, j)),
                  pl.BlockSpec((None, None, d, tf), lambda b, i, j: (l, sub, 0, nf + j))],
        out_specs=pl.BlockSpec((bb * tt, tf), lambda b, i, j: (b * nt + i, j)),
        scratch_shapes=[pltpu.VMEM((bb * tt, d), BF)],
        compiler_params=_params("parallel", "parallel", "arbitrary"),
        name="ffn_up",
    )(x, mod5, mod5, norm4, w_in, w_in)
    return pl.pallas_call(
        _ffn_down_kernel,
        out_shape=jax.ShapeDtypeStruct(x.shape, F32),
        grid=(bsz // bb, nt, d // tn),
        in_specs=[pl.BlockSpec((bb * tt, f), lambda b, i, j: (b * nt + i, 0)),
                  pl.BlockSpec((bb, tt, tn), lambda b, i, j: (b, i, j)),
                  pl.BlockSpec((None, bb, None, 1, tn), lambda b, i, j: (l, b, mod_base + 2, 0, j)),
                  pl.BlockSpec((None, None, f, tn), lambda b, i, j: (l, sub, 0, j))],
        out_specs=pl.BlockSpec((bb, tt, tn), lambda b, i, j: (b, i, j)),
        compiler_params=_params("parallel", "parallel", "arbitrary"),
        name="ffn_down",
    )(act, x, mod5, w_out)


def _glu_kernel(x_ref, sh_ref, sc_ref, g_ref, wa_ref, wg_ref, ba_ref, bg_ref, o_ref, h_ref):
    j = pl.program_id(2)
    bb, tt, _ = x_ref.shape

    @pl.when(j == 0)
    def _():
        _modulate_into(h_ref, x_ref, g_ref, sh_ref, sc_ref)

    h = h_ref[...]
    a = _dot(h, wa_ref[...]) + ba_ref[...]
    g = _dot(h, wg_ref[...]) + bg_ref[...]
    o_ref[...] = (a * _sigmoid(g)).reshape(bb, tt, a.shape[-1])


def _glu(x, mod5, norm4, w_in, b_in, l, bb, tt, tn):
    bsz, t, d = x.shape
    nn = d // tn
    b3 = b_in.reshape(b_in.shape[0], 1, 2 * d)
    return pl.pallas_call(
        _glu_kernel,
        out_shape=jax.ShapeDtypeStruct(x.shape, F32),
        grid=(bsz // bb, t // tt, nn),
        in_specs=[_x_spec(bb, tt, d),
                  _mod_spec(bb, d, l, 3), _mod_spec(bb, d, l, 4),
                  _gain_spec(d, l, 1),
                  pl.BlockSpec((None, d, tn), lambda b, i, j: (l, 0, j)),
                  pl.BlockSpec((None, d, tn), lambda b, i, j: (l, 0, nn + j)),
                  pl.BlockSpec((None, 1, tn), lambda b, i, j: (l, 0, j)),
                  pl.BlockSpec((None, 1, tn), lambda b, i, j: (l, 0, nn + j))],
        out_specs=pl.BlockSpec((bb, tt, tn), lambda b, i, j: (b, i, j)),
        scratch_shapes=[pltpu.VMEM((bb * tt, d), BF)],
        compiler_params=_params("parallel", "parallel", "arbitrary"),
        name="conv_glu",
    )(x, mod5, mod5, norm4, w_in, w_in, b3, b3)


def _conv_tail_kernel(u_ref, halo_ref, hist_ref, x_ref, gt_ref, dw_ref, dwb_ref, lng_ref, lnb_ref,
                      wo_ref, bo_ref, o_ref, ext_ref, z_ref, *, rows_per_step):
    i = pl.program_id(1)
    j = pl.program_id(2)
    bb, tt, d = u_ref.shape
    rps = rows_per_step

    @pl.when(j == 0)
    def _():
        ext_ref[:, 0:HALO, :] = jnp.where(i == 0, hist_ref[...], halo_ref[...])
        ext_ref[:, HALO:HALO + tt, :] = u_ref[...]
        off = HALO - (CONV_W - 1)

        def rows(c, carry):
            r0 = pl.multiple_of(c * rps, rps)
            parts = []
            for c0 in range(0, d, HEAD_DIM):
                lanes = slice(c0, c0 + HEAD_DIM)
                win = ext_ref[:, pl.ds(r0, rps + HALO), lanes]
                part = jnp.zeros((bb, rps, HEAD_DIM), F32) + dwb_ref[:, lanes]
                for s in range(SUBLANES):
                    taps = [w for w in range(CONV_W) if (off + w) % SUBLANES == s]
                    shifted = win if s == 0 else win[:, s:s + rps + HALO - SUBLANES, :]
                    for w in taps:
                        a0 = (off + w) // SUBLANES * SUBLANES
                        part = part + dw_ref[w:w + 1, lanes] * shifted[:, a0:a0 + rps, :]
                parts.append(part)
            acc = jnp.concatenate(parts, axis=-1)
            mu = jnp.mean(acc, axis=-1, keepdims=True)
            cen = acc - mu
            var = jnp.mean(cen * cen, axis=-1, keepdims=True)
            y = cen * lax.rsqrt(var + EPS) * lng_ref[...] + lnb_ref[...]
            z = y * _sigmoid(y)
            for bi in range(bb):
                z_ref[pl.ds(bi * tt + r0, rps), :] = z[bi]
            return carry

        lax.fori_loop(0, tt // rps, rows, 0)

    out = _dot(z_ref[...].astype(BF), wo_ref[...]) + bo_ref[...]
    o_ref[...] = x_ref[...] + gt_ref[...] * out.reshape(bb, tt, out.shape[-1])


def _conv_tail(u, hist_pad, x, mod5, dw, dw_b, ln_g, ln_b, w_out, b_out, l, bb, tt, tn):
    bsz, t, d = x.shape
    nl = dw.shape[0]
    if t > tt:
        per = tt // HALO
        halo_arr = u
        halo_spec = pl.BlockSpec((bb, HALO, d), lambda b, i, j: (b, jnp.maximum(i * per - 1, 0), 0))
    else:
        halo_arr = hist_pad
        halo_spec = pl.BlockSpec((bb, HALO, d), lambda b, i, j: (b, 0, 0))
    vec = lambda a: a.reshape(nl, 1, d)
    row_spec = pl.BlockSpec((None, 1, d), lambda b, i, j: (l, 0, 0))
    return pl.pallas_call(
        functools.partial(_conv_tail_kernel, rows_per_step=min(tt, 32)),
        out_shape=jax.ShapeDtypeStruct(x.shape, F32),
        grid=(bsz // bb, t // tt, d // tn),
        in_specs=[_x_spec(bb, tt, d),
                  halo_spec,
                  pl.BlockSpec((bb, HALO, d), lambda b, i, j: (b, 0, 0)),
                  pl.BlockSpec((bb, tt, tn), lambda b, i, j: (b, i, j)),
                  pl.BlockSpec((None, bb, None, 1, tn), lambda b, i, j: (l, b, 5, 0, j)),
                  pl.BlockSpec((None, CONV_W, d), lambda b, i, j: (l, 0, 0)),
                  row_spec, row_spec, row_spec,
                  pl.BlockSpec((None, d, tn), lambda b, i, j: (l, 0, j)),
                  pl.BlockSpec((None, 1, tn), lambda b, i, j: (l, 0, j))],
        out_specs=pl.BlockSpec((bb, tt, tn), lambda b, i, j: (b, i, j)),
        scratch_shapes=[pltpu.VMEM((bb, HALO + tt, d), F32), pltpu.VMEM((bb * tt, d), F32)],
        compiler_params=_params("parallel", "parallel", "arbitrary"),
        name="conv_tail",
    )(u, halo_arr, hist_pad, x, mod5, dw, vec(dw_b), vec(ln_g), vec(ln_b), w_out, vec(b_out))


def _kv_kernel(x_ref, sh_ref, sc_ref, g_ref, w_ref, kg_ref, o_ref, h_ref):
    j = pl.program_id(2)
    bb, tt, _ = x_ref.shape

    @pl.when(j == 0)
    def _():
        _modulate_into(h_ref, x_ref, g_ref, sh_ref, sc_ref)

    y = _dot(h_ref[...], w_ref[...])
    normed = _rms_groups(y, kg_ref[...])
    y = jnp.where((j == 2) | (j == 4), normed, y)
    o_ref[...] = y.reshape(bb, tt, y.shape[-1])


def _kv_proj(x, kvmod5, kv_norm_g, w_kv, k_norm_g, bb, tt):
    bsz, t, d = x.shape
    n = w_kv.shape[1]
    ns = n // KV_SLOT
    gains = jnp.ones((ns, 1, KV_SLOT), F32)
    gains = gains.at[2, 0].set(jnp.tile(k_norm_g[1], N_KV)).at[4, 0].set(jnp.tile(k_norm_g[2], N_KV))
    return pl.pallas_call(
        _kv_kernel,
        out_shape=jax.ShapeDtypeStruct((bsz, t, n), F32),
        grid=(bsz // bb, t // tt, ns),
        in_specs=[_x_spec(bb, tt, d),
                  _mod_spec(bb, d, 0, 0), _mod_spec(bb, d, 0, 1),
                  pl.BlockSpec((1, d), lambda b, i, j: (0, 0)),
                  pl.BlockSpec((d, KV_SLOT), lambda b, i, j: (0, j)),
                  pl.BlockSpec((None, 1, KV_SLOT), lambda b, i, j: (j, 0, 0))],
        out_specs=pl.BlockSpec((bb, tt, KV_SLOT), lambda b, i, j: (b, i, j)),
        scratch_shapes=[pltpu.VMEM((bb * tt, d), BF)],
        compiler_params=_params("parallel", "parallel", "arbitrary"),
        name="kv_proj",
    )(x, kvmod5, kvmod5, kv_norm_g.reshape(1, d), w_kv, gains)


def _q_kernel(x_ref, sh_ref, sc_ref, g_ref, wq_ref, wg_ref, qg_ref, q_ref, gate_ref, h_ref):
    j = pl.program_id(2)
    bb, tt, _ = x_ref.shape

    @pl.when(j == 0)
    def _():
        _modulate_into(h_ref, x_ref, g_ref, sh_ref, sc_ref)
        gl = _dot(h_ref[...], wg_ref[...])
        gate_ref[...] = _sigmoid(gl).reshape(bb, tt, gl.shape[-1])

    y = _dot(h_ref[...], wq_ref[...])
    q_ref[...] = _rms_groups(y, qg_ref[...]).reshape(bb, tt, y.shape[-1])


def _q_proj(x, mod5, norm4, w_q, w_gate, q_gain, l, lj, bb, tt):
    bsz, t, d = x.shape
    hd = w_q.shape[2]
    ng = w_gate.shape[2]
    return pl.pallas_call(
        _q_kernel,
        out_shape=(jax.ShapeDtypeStruct((bsz, t, hd), F32), jax.ShapeDtypeStruct((bsz, t, ng), F32)),
        grid=(bsz // bb, t // tt, hd // KV_SLOT),
        in_specs=[_x_spec(bb, tt, d),
                  _mod_spec(bb, d, l, 3), _mod_spec(bb, d, l, 4),
                  _gain_spec(d, l, 1),
                  pl.BlockSpec((None, d, KV_SLOT), lambda b, i, j: (lj, 0, j)),
                  pl.BlockSpec((None, d, ng), lambda b, i, j: (lj, 0, 0)),
                  pl.BlockSpec((None, 1, KV_SLOT), lambda b, i, j: (lj, 0, 0))],
        out_specs=(pl.BlockSpec((bb, tt, KV_SLOT), lambda b, i, j: (b, i, j)),
                   pl.BlockSpec((bb, tt, ng), lambda b, i, j: (b, i, 0))),
        scratch_shapes=[pltpu.VMEM((bb * tt, d), BF)],
        compiler_params=_params("parallel", "parallel", "arbitrary"),
        name="q_proj",
    )(x, mod5, mod5, norm4, w_q, w_gate, q_gain)


def _oproj_kernel(a_ref, x_ref, gt_ref, w_ref, o_ref):
    bb, tt, k = a_ref.shape
    out = _dot(a_ref[...].reshape(bb * tt, k).astype(BF), w_ref[...])
    o_ref[...] = x_ref[...] + gt_ref[...] * out.reshape(bb, tt, out.shape[-1])


def _o_proj(a, x, mod5, w_o, l, lj, bb, tt, tn):
    bsz, t, d = x.shape
    k = a.shape[2]
    return pl.pallas_call(
        _oproj_kernel,
        out_shape=jax.ShapeDtypeStruct(x.shape, F32),
        grid=(bsz // bb, t // tt, d // tn),
        in_specs=[_x_spec(bb, tt, k),
                  pl.BlockSpec((bb, tt, tn), lambda b, i, j: (b, i, j)),
                  pl.BlockSpec((None, bb, None, 1, tn), lambda b, i, j: (l, b, 5, 0, j)),
                  pl.BlockSpec((None, k, tn), lambda b, i, j: (lj, 0, j))],
        out_specs=pl.BlockSpec((bb, tt, tn), lambda b, i, j: (b, i, j)),
        compiler_params=_params("parallel", "parallel", "arbitrary"),
        name="o_proj",
    )(a, x, mod5, w_o)


def _compress_pair(rows_a, rows_b, j, pos_ref, w1_ref):
    lhs = jnp.concatenate(
        [jnp.concatenate([rows_a[g] + pos_ref[j:j + 1, :], rows_b[g] + pos_ref[j + 1:j + 2, :]], axis=-1)
         for g in range(N_KV)], axis=0)
    return _dot(lhs.astype(BF), w1_ref[j * HEAD_DIM:(j + 2) * HEAD_DIM, :])


def _compress_finish(acc, nb, b1_ref, w2_ref, kg_ref, is_key):
    hid = jax.nn.gelu(acc + b1_ref[...])
    out = _dot(hid.astype(BF), w2_ref[...].astype(BF))
    normed = out * lax.rsqrt(jnp.mean(out * out, axis=-1, keepdims=True) + EPS) * kg_ref[...]
    out = jnp.where(is_key, normed, out)
    return out.reshape(N_KV, nb, HEAD_DIM)


def _compress_prompt_kernel(r0_ref, r1_ref, r2_ref, r3_ref, pos_ref, w1_ref, b1_ref, w2_ref, kg_ref, o_ref):
    heads = [r0_ref, r1_ref, r2_ref, r3_ref]
    nb = r0_ref.shape[0] // BLK
    acc = jnp.zeros((N_KV * nb, w1_ref.shape[-1]), F32)
    for j in range(0, BLK, 2):
        acc = acc + _compress_pair([h[pl.ds(j, nb, stride=BLK), :] for h in heads],
                                   [h[pl.ds(j + 1, nb, stride=BLK), :] for h in heads], j, pos_ref, w1_ref)
    o_ref[...] = _compress_finish(acc, nb, b1_ref, w2_ref, kg_ref, pl.program_id(0) == 0)


def _compress_prompt(kv_all, pos_t, w1, b1, w2, kgain):
    bsz, t, _ = kv_all.shape
    nc = t // BLK
    kf = BLK * HEAD_DIM

    def head_spec(g):
        return pl.BlockSpec((None, nc * BLK, HEAD_DIM), lambda p, b: (b, 0, p * N_KV + g))

    return pl.pallas_call(
        _compress_prompt_kernel,
        out_shape=jax.ShapeDtypeStruct((2, bsz, N_KV, nc, HEAD_DIM), F32),
        grid=(2, bsz),
        in_specs=[head_spec(g) for g in range(N_KV)] + [
            pl.BlockSpec((None, BLK, HEAD_DIM), lambda p, b: (p, 0, 0)),
            pl.BlockSpec((None, kf, 2 * HEAD_DIM), lambda p, b: (p, 0, 0)),
            pl.BlockSpec((None, 1, 2 * HEAD_DIM), lambda p, b: (p, 0, 0)),
            pl.BlockSpec((None, 2 * HEAD_DIM, HEAD_DIM), lambda p, b: (p, 0, 0)),
            pl.BlockSpec((1, HEAD_DIM), lambda p, b: (0, 0))],
        out_specs=pl.BlockSpec((None, None, N_KV, nc, HEAD_DIM), lambda p, b: (p, b, 0, 0, 0)),
        compiler_params=_params("arbitrary", "arbitrary"),
        name="compress_prompt",
    )(kv_all, kv_all, kv_all, kv_all, pos_t, w1, b1.reshape(2, 1, -1), w2, kgain)


def _compress_paged_kernel(pt_ref, cache_ref, w1_hbm, pos_ref, b1_ref, w2_ref, kg_ref, o_ref,
                           rows_ref, w1_ref, sem, w1_sem):
    b = pl.program_id(0)
    s = pl.program_id(1)
    ns = pl.num_programs(1)
    step = b * ns + s
    n_steps = pl.num_programs(0) * ns
    page = cache_ref.shape[1]
    n_rows = rows_ref.shape[0] // 2
    n_pg = n_rows // page
    nb = n_rows // BLK
    n_sub = 2 * N_KV
    slot = step % 2
    base = pl.multiple_of(slot * n_rows, n_rows)

    def page_copy(bi, si, k, sl):
        src = cache_ref.at[pt_ref[bi, si * n_pg + k], :, pl.ds(0, n_sub), :]
        dst = rows_ref.at[pl.ds(pl.multiple_of(sl * n_rows + k * page, page), page), :, :]
        return pltpu.make_async_copy(src, dst, sem.at[sl])

    def start_step(bi, si, sl):
        def body(k, c):
            page_copy(bi, si, k, sl).start()
            return c
        lax.fori_loop(0, n_pg, body, 0)

    @pl.when(step == 0)
    def _():
        w1_copy = pltpu.make_async_copy(w1_hbm, w1_ref, w1_sem)
        w1_copy.start()
        start_step(b, s, slot)
        w1_copy.wait()

    @pl.when(step + 1 < n_steps)
    def _():
        nxt = step + 1
        start_step(nxt // ns, nxt % ns, 1 - slot)

    def wait_body(k, c):
        page_copy(b, s, k, slot).wait()
        return c

    lax.fori_loop(0, n_pg, wait_body, 0)

    def block_rows(j):
        return jnp.swapaxes(rows_ref[pl.ds(base + j, nb, stride=BLK), :, :], 0, 1)

    accs = [jnp.zeros((N_KV * nb, w1_ref.shape[-1]), F32) for _ in range(2)]
    for j in range(0, BLK, 2):
        xa = block_rows(j)
        xb = block_rows(j + 1)
        for p in range(2):
            heads = range(p * N_KV, (p + 1) * N_KV)
            accs[p] = accs[p] + _compress_pair([xa[h] for h in heads], [xb[h] for h in heads], j,
                                               pos_ref.at[p], w1_ref.at[p])
    for p in range(2):
        o_ref[p] = _compress_finish(accs[p], nb, b1_ref.at[p], w2_ref.at[p], kg_ref, p == 0)


def _compress_paged(cache4, page_table, pos_t, w1, b1, w2, kgain, pages_per_step):
    bsz, n_pages = page_table.shape
    page = cache4.shape[1]
    nb = pages_per_step * page // BLK
    nc = n_pages * page // BLK
    full = lambda a: pl.BlockSpec(a.shape, lambda b, s, pt: (0,) * a.ndim)
    b1r = b1.reshape(2, 1, -1)
    grid_spec = pltpu.PrefetchScalarGridSpec(
        num_scalar_prefetch=1,
        grid=(bsz, n_pages // pages_per_step),
        in_specs=[pl.BlockSpec(memory_space=pl.ANY), pl.BlockSpec(memory_space=pl.ANY),
                  full(pos_t), full(b1r), full(w2), full(kgain)],
        out_specs=pl.BlockSpec((2, None, N_KV, nb, HEAD_DIM), lambda b, s, pt: (0, b, 0, s, 0)),
        scratch_shapes=[pltpu.VMEM((2 * pages_per_step * page, 2 * N_KV, HEAD_DIM), F32),
                        pltpu.VMEM(w1.shape, w1.dtype),
                        pltpu.SemaphoreType.DMA((2,)), pltpu.SemaphoreType.DMA],
    )
    return pl.pallas_call(
        _compress_paged_kernel,
        out_shape=jax.ShapeDtypeStruct((2, bsz, N_KV, nc, HEAD_DIM), F32),
        grid_spec=grid_spec,
        compiler_params=_params("arbitrary", "arbitrary"),
        name="compress_paged",
    )(page_table, cache4, w1, pos_t, b1r, w2, kgain)


def _stack_heads(q_val):
    return jnp.concatenate([q_val[:, r * HEAD_DIM:(r + 1) * HEAD_DIM] for r in range(GROUP)], axis=0)


def _masked_scores(q, k_bf, valid, dist, slopes_ref, g):
    t = dist.shape[0]
    raw = _dot_nt(q, k_bf)
    return jnp.concatenate(
        [jnp.where(valid, raw[r * t:(r + 1) * t] * SCALE - slopes_ref[g, r] * dist, NEG) for r in range(GROUP)],
        axis=0)


def _softmax_full(s):
    m = jnp.maximum(jnp.max(s, axis=-1, keepdims=True), M_FLOOR)
    e = jnp.exp(s - m)
    return e, jnp.sum(e, axis=-1, keepdims=True)


def _online_update(s, v_bf, m_old, l_old, acc_old):
    m_new = jnp.maximum(m_old, jnp.max(s, axis=-1, keepdims=True))
    e = jnp.exp(s - jnp.maximum(m_new, M_FLOOR))
    alpha = jnp.exp(m_old - m_new)
    l_new = alpha * l_old + jnp.sum(e, axis=-1, keepdims=True)
    acc_new = alpha * acc_old + _dot(e.astype(BF), v_bf)
    return m_new, l_new, acc_new


def _head_sum(p, t):
    out = p[0:t]
    for r in range(1, GROUP):
        out = out + p[r * t:(r + 1) * t]
    return out


def _top_blocks(score, blk, k):
    n = score.shape[-1]
    sel = jnp.zeros(score.shape, F32)
    for _ in range(k):
        mx = jnp.max(score, axis=-1, keepdims=True)
        idx = jnp.min(jnp.where(score == mx, blk, float(n)), axis=-1, keepdims=True)
        hit = blk == idx
        sel = jnp.where(hit & (mx >= 0.0), 1.0, sel)
        score = jnp.where(hit, -2.0, score)
    return sel


def _rank_select_t(score_t, k):
    n, t = score_t.shape
    groups = [score_t[r0:r0 + SUBLANES] for r0 in range(0, n, SUBLANES)]
    ranks = [jnp.zeros((SUBLANES, t), F32) for _ in groups]
    sub = lax.broadcasted_iota(jnp.int32, (SUBLANES, t), 0)
    for j in range(n):
        sj = score_t[j:j + 1, :]
        for gi, grp in enumerate(groups):
            lo = gi * SUBLANES
            ge = jnp.where(sj >= grp, 1.0, 0.0)
            gt = jnp.where(sj > grp, 1.0, 0.0)
            if lo > j:
                beats = ge
            elif lo + SUBLANES - 1 <= j:
                beats = gt
            else:
                beats = jnp.where(sub > j - lo, ge, gt)
            ranks[gi] = ranks[gi] + beats
    rank = jnp.concatenate(ranks, axis=0)
    return jnp.where((rank < float(k)) & (score_t >= 0.0), 1.0, 0.0)


def _expand_matrix(nsel, first_blk, n_keys):
    row = lax.broadcasted_iota(jnp.int32, (nsel, n_keys), 0)
    col = lax.broadcasted_iota(jnp.int32, (nsel, n_keys), 1)
    return jnp.where(row == first_blk + col // BLK, 1.0, 0.0).astype(BF)


def _combine(gate, o_cmp, o_sel, o_win, t):
    cols = []
    for r in range(GROUP):
        rs = slice(r * t, (r + 1) * t)
        cols.append(gate[:, r:r + 1] * o_cmp[rs] + gate[:, GROUP + r:GROUP + r + 1] * o_sel[rs]
                    + gate[:, 2 * GROUP + r:2 * GROUP + r + 1] * o_win[rs])
    return jnp.concatenate(cols, axis=-1)


def _nsa_prompt_kernel(slopes_ref, q_ref, gate_ref, kc_ref, vc_ref, ks_ref, vs_ref, kw_ref, vw_ref, relb_ref,
                       o_ref, m_ref, l_ref, acc_ref, *, tk):
    g = pl.program_id(1)
    qi = pl.program_id(2)
    tq = q_ref.shape[0]
    nc = kc_ref.shape[0]
    rows = GROUP * tq
    t0 = qi * tq
    q = _stack_heads(q_ref[...]).astype(BF)
    c1 = SCALE * LOG2E

    blk_t = lax.broadcasted_iota(jnp.int32, (nc, tq), 0)
    tok_t = t0 + lax.broadcasted_iota(jnp.int32, (nc, tq), 1)
    dist_t = (tok_t - (blk_t * BLK + (BLK - 1))).astype(F32)
    valid_t = dist_t >= 0
    raw_t = _dot_nt(kc_ref[...].astype(BF), q)
    p_heads = []
    for r in range(GROUP):
        s = jnp.where(valid_t, raw_t[:, r * tq:(r + 1) * tq] * SCALE - slopes_ref[g, r] * dist_t, NEG)
        m = jnp.maximum(jnp.max(s, axis=0, keepdims=True), M_FLOOR)
        e = jnp.exp(s - m)
        p_heads.append(e / jnp.maximum(jnp.sum(e, axis=0, keepdims=True), 1e-30))
    o_cmp = _dot_tn(jnp.concatenate(p_heads, axis=1).astype(BF), vc_ref[...].astype(BF))
    imp_t = p_heads[0]
    for r in range(1, GROUP):
        imp_t = imp_t + p_heads[r]

    cur_t = tok_t // BLK
    forced = (blk_t == 0) | (blk_t == cur_t) | (blk_t == cur_t - 1)
    score_t = jnp.where(forced, FORCE, imp_t)
    score_t = jnp.where(blk_t <= cur_t, score_t, -1.0)
    sel_t = _rank_select_t(score_t, min(TOP_N, nc))
    unsel = ((1.0 - sel_t) * BIG).T.astype(BF)

    m_ref[...] = jnp.full(m_ref.shape, NEG, F32)
    l_ref[...] = jnp.zeros(l_ref.shape, F32)
    acc_ref[...] = jnp.zeros(acc_ref.shape, F32)
    rel = lax.broadcasted_iota(jnp.int32, (tq, tk), 0) - lax.broadcasted_iota(jnp.int32, (tq, tk), 1)
    slope_col = jnp.concatenate([jnp.full((tq, 1), slopes_ref[g, r] * LOG2E, F32) for r in range(GROUP)], axis=0)

    def key_tile(kt, carry):
        k0 = pl.multiple_of(kt * tk, tk)
        raw = _dot_nt(q, ks_ref[pl.ds(k0, tk), :].astype(BF))
        mask = _dot(unsel, _expand_matrix(nc, kt * (tk // BLK), tk))
        mask = mask + jnp.where(rel + (t0 - k0) >= 0, 0.0, BIG)
        s = jnp.concatenate([raw[r * tq:(r + 1) * tq] * c1 - relb_ref[r * tq:(r + 1) * tq, 0:tk] - mask
                             for r in range(GROUP)], axis=0)
        col = slope_col * (t0 - k0).astype(F32)
        m_old = m_ref[...]
        m_new = jnp.maximum(m_old, jnp.max(s, axis=-1, keepdims=True) - col)
        e = jnp.exp2(s - (jnp.maximum(m_new, M_FLOOR) + col))
        alpha = jnp.exp2(m_old - m_new)
        l_ref[...] = alpha * l_ref[...] + jnp.sum(e, axis=-1, keepdims=True)
        acc_ref[...] = alpha * acc_ref[...] + _dot(e.astype(BF), vs_ref[pl.ds(k0, tk), :].astype(BF))
        m_ref[...] = m_new
        return carry

    lax.fori_loop(0, (t0 + tq + tk - 1) // tk, key_tile, 0)
    o_sel = acc_ref[...] / jnp.maximum(l_ref[...], 1e-30)

    lw = WINDOW + tq
    w0 = pl.multiple_of(jnp.maximum(t0 - WINDOW, 0), SUBLANES)
    dist = (t0 - w0) + (lax.broadcasted_iota(jnp.int32, (tq, lw), 0) - lax.broadcasted_iota(jnp.int32, (tq, lw), 1))
    mask = jnp.where((dist >= 0) & (dist < WINDOW), 0.0, BIG)
    raw = _dot_nt(q, kw_ref[pl.ds(w0, lw), :].astype(BF))
    s = jnp.concatenate([raw[r * tq:(r + 1) * tq] * c1 - relb_ref[r * tq:(r + 1) * tq, 0:lw] - mask
                         for r in range(GROUP)], axis=0)
    m = jnp.maximum(jnp.max(s, axis=-1, keepdims=True), M_FLOOR)
    e = jnp.exp2(s - m)
    o_win = _dot(e.astype(BF), vw_ref[pl.ds(w0, lw), :].astype(BF)) / jnp.maximum(
        jnp.sum(e, axis=-1, keepdims=True), 1e-30)

    o_ref[...] = _combine(gate_ref[...], o_cmp, o_sel, o_win, tq)


def _nsa_prompt(q, gates, cmp_kv, kv_all, slopes, tq, tk):
    bsz, t, hd = q.shape
    nc = cmp_kv.shape[3]
    lw = WINDOW + tq
    assert t % BLK == 0 and t % tk == 0 and t >= lw and tk <= lw
    rows = GROUP * tq
    rel = (jnp.arange(tq, dtype=F32)[:, None] - jnp.arange(lw, dtype=F32)[None, :])
    relb = ((slopes * LOG2E)[:, :, None, None] * rel[None, None]).reshape(N_KV, rows, lw)

    def col(slot):
        return pl.BlockSpec((None, t, HEAD_DIM), lambda b, g, i: (b, 0, slot * N_KV + g))

    def cmp_spec(p):
        return pl.BlockSpec((None, None, None, nc, HEAD_DIM), lambda b, g, i: (p, b, g, 0, 0))

    return pl.pallas_call(
        functools.partial(_nsa_prompt_kernel, tk=tk),
        out_shape=jax.ShapeDtypeStruct(q.shape, F32),
        grid=(bsz, N_KV, t // tq),
        in_specs=[pl.BlockSpec(memory_space=pltpu.SMEM),
                  pl.BlockSpec((None, tq, GROUP * HEAD_DIM), lambda b, g, i: (b, i, g)),
                  pl.BlockSpec((None, tq, HEAD_DIM), lambda b, g, i: (b, i, g)),
                  cmp_spec(0), cmp_spec(1), col(2), col(3), col(4), col(5),
                  pl.BlockSpec((None, rows, lw), lambda b, g, i: (g, 0, 0))],
        out_specs=pl.BlockSpec((None, tq, GROUP * HEAD_DIM), lambda b, g, i: (b, i, g)),
        scratch_shapes=[pltpu.VMEM((rows, 1), F32), pltpu.VMEM((rows, 1), F32),
                        pltpu.VMEM((rows, HEAD_DIM), F32)],
        compiler_params=_params("parallel", "parallel", "arbitrary"),
        name="nsa_prompt",
    )(slopes, q, gates, cmp_kv, cmp_kv, kv_all, kv_all, kv_all, kv_all, relb)


def _nsa_paged_kernel(pt_ref, slopes_ref, q_ref, gate_ref, kc_ref, vc_ref, new_ref, cwin_ref, *rest,
                      n_pg, past_len):
    page_refs = rest[:n_pg]
    o_ref, sel_ref, m_ref, l_ref, acc_ref, ocmp_ref, owin_ref, pad_ref = rest[n_pg:]
    s_idx = pl.program_id(1)
    tq = q_ref.shape[0]
    nc = kc_ref.shape[1]
    page = page_refs[0].shape[0]
    rows = GROUP * tq
    npad = pad_ref.shape[0]
    wlen = cwin_ref.shape[0]

    def q_rows(g):
        return _stack_heads(q_ref[:, g * KV_SLOT:(g + 1) * KV_SLOT]).astype(BF)

    def tok_pos(n):
        return past_len + lax.broadcasted_iota(jnp.int32, (tq, n), 0)

    def new_cols(slot, g):
        return pad_ref[:, slot * KV_SLOT + g * HEAD_DIM:slot * KV_SLOT + (g + 1) * HEAD_DIM]

    def by_head(ref):
        return jnp.swapaxes(ref[...], 0, 1)

    @pl.when(s_idx == 0)
    def _():
        pad_ref[...] = jnp.zeros(pad_ref.shape, F32)
        pad_ref[0:tq, :] = new_ref[...]
        new_i = lax.broadcasted_iota(jnp.int32, (tq, npad), 1)
        blk_i = lax.broadcasted_iota(jnp.int32, (tq, nc), 1)
        cwin = by_head(cwin_ref)
        imps = []
        for g in range(N_KV):
            q = q_rows(g)
            dist = (tok_pos(nc) - (blk_i * BLK + (BLK - 1))).astype(F32)
            e, l = _softmax_full(_masked_scores(q, kc_ref[g].astype(BF), dist >= 0, dist, slopes_ref, g))
            p_cmp = e / jnp.maximum(l, 1e-30)
            ocmp_ref[g] = _dot(p_cmp.astype(BF), vc_ref[g].astype(BF))
            imps.append(_head_sum(p_cmp, tq))
            dist = (tok_pos(npad) - (past_len + new_i)).astype(F32)
            valid = (dist >= 0) & (new_i < tq)
            s = _masked_scores(q, new_cols(2, g).astype(BF), valid, dist, slopes_ref, g)
            m, l, acc = _online_update(s, new_cols(3, g).astype(BF), jnp.full((rows, 1), NEG, F32),
                                       jnp.zeros((rows, 1), F32), jnp.zeros((rows, HEAD_DIM), F32))
            m_ref[g] = m
            l_ref[g] = l
            acc_ref[g] = acc
            kw = jnp.concatenate([cwin[g], new_cols(4, g)], axis=0)
            vw = jnp.concatenate([cwin[N_KV + g], new_cols(5, g)], axis=0)
            win_i = lax.broadcasted_iota(jnp.int32, (tq, wlen + npad), 1)
            dist = (tok_pos(wlen + npad) - (past_len - wlen + win_i)).astype(F32)
            valid = (dist >= 0) & (dist < WINDOW) & (win_i < wlen + tq)
            e, l = _softmax_full(_masked_scores(q, kw.astype(BF), valid, dist, slopes_ref, g))
            owin_ref[g] = _dot(e.astype(BF), vw.astype(BF)) / jnp.maximum(l, 1e-30)
        blk_all = lax.broadcasted_iota(jnp.int32, (N_KV * tq, nc), 1)
        score = jnp.where((blk_all == 0) | (blk_all == nc - 1), FORCE, jnp.concatenate(imps, axis=0))
        sel = _top_blocks(score, blk_all.astype(F32), min(TOP_N, nc + 1) - 1)
        for g in range(N_KV):
            sel_ref[g] = sel[g * tq:(g + 1) * tq]

    nk = n_pg * page
    k0 = s_idx * nk
    kpos = k0 + lax.broadcasted_iota(jnp.int32, (tq, nk), 1)
    dist = (tok_pos(nk) - kpos).astype(F32)
    expand = _expand_matrix(nc, s_idx * (nk // BLK), nk)
    pages = [by_head(pr) for pr in page_refs]
    for g in range(N_KV):
        q = q_rows(g)
        ks = jnp.concatenate([pg[g] for pg in pages], axis=0)
        vs = jnp.concatenate([pg[N_KV + g] for pg in pages], axis=0)
        valid = (_dot(sel_ref[g].astype(BF), expand) > 0.5) & (dist >= 0)
        s = _masked_scores(q, ks.astype(BF), valid, dist, slopes_ref, g)
        m, l, acc = _online_update(s, vs.astype(BF), m_ref[g], l_ref[g], acc_ref[g])
        m_ref[g] = m
        l_ref[g] = l
        acc_ref[g] = acc

    @pl.when(s_idx == pl.num_programs(1) - 1)
    def _():
        outs = []
        for g in range(N_KV):
            o_sel = acc_ref[g] / jnp.maximum(l_ref[g], 1e-30)
            outs.append(_combine(gate_ref[:, g * HEAD_DIM:(g + 1) * HEAD_DIM], ocmp_ref[g], o_sel, owin_ref[g], tq))
        o_ref[...] = jnp.concatenate(outs, axis=-1)


def _nsa_paged(q, gates, cmp_kv, kv_new, cache4, cache_win3, page_table, slopes, n_pg):
    bsz, tq, hd = q.shape
    n_pages = page_table.shape[1]
    page = cache4.shape[1]
    past_len = n_pages * page
    nc = cmp_kv.shape[3]
    wlen = cache_win3.shape[1]
    assert past_len % BLK == 0 and tq < BLK and nc == past_len // BLK and n_pages % n_pg == 0
    assert wlen == WINDOW and tq % SUBLANES == 0
    rows = GROUP * tq
    npad = 128

    def cmp_spec(p):
        return pl.BlockSpec((None, None, N_KV, nc, HEAD_DIM), lambda b, s, pt: (p, b, 0, 0, 0))

    def page_spec(k):
        return pl.BlockSpec((None, page, 2 * N_KV, HEAD_DIM), lambda b, s, pt: (pt[b, s * n_pg + k], 0, 1, 0))

    grid_spec = pltpu.PrefetchScalarGridSpec(
        num_scalar_prefetch=1,
        grid=(bsz, n_pages // n_pg),
        in_specs=[pl.BlockSpec(memory_space=pltpu.SMEM),
                  pl.BlockSpec((None, tq, hd), lambda b, s, pt: (b, 0, 0)),
                  pl.BlockSpec((None, tq, gates.shape[2]), lambda b, s, pt: (b, 0, 0)),
                  cmp_spec(0), cmp_spec(1),
                  pl.BlockSpec((None, tq, kv_new.shape[2]), lambda b, s, pt: (b, 0, 0)),
                  pl.BlockSpec((None, wlen, 2 * N_KV, HEAD_DIM), lambda b, s, pt: (b, 0, 0, 0))]
                 + [page_spec(k) for k in range(n_pg)],
        out_specs=pl.BlockSpec((None, tq, hd), lambda b, s, pt: (b, 0, 0)),
        scratch_shapes=[pltpu.VMEM((N_KV, tq, nc), F32),
                        pltpu.VMEM((N_KV, rows, 1), F32), pltpu.VMEM((N_KV, rows, 1), F32),
                        pltpu.VMEM((N_KV, rows, HEAD_DIM), F32),
                        pltpu.VMEM((N_KV, rows, HEAD_DIM), F32), pltpu.VMEM((N_KV, rows, HEAD_DIM), F32),
                        pltpu.VMEM((npad, kv_new.shape[2]), F32)],
    )
    return pl.pallas_call(
        functools.partial(_nsa_paged_kernel, n_pg=n_pg, past_len=past_len),
        out_shape=jax.ShapeDtypeStruct(q.shape, F32),
        grid_spec=grid_spec,
        compiler_params=_params("parallel", "arbitrary"),
        name="nsa_paged",
    )(page_table, slopes, q, gates, cmp_kv, cmp_kv, kv_new, cache_win3, *([cache4] * n_pg))


def _alibi_slopes():
    n_heads = N_KV * GROUP
    m = np.exp2(-8.0 * np.arange(1, n_heads + 1) / n_heads)
    return jnp.asarray(m, dtype=F32).reshape(N_KV, GROUP)


def _gate_weights(w_qg, hd):
    nl, d, _ = w_qg.shape
    wg = w_qg[:, :, hd:].reshape(nl, d, 3, N_KV, GROUP).transpose(0, 1, 3, 2, 4).reshape(nl, d, N_KV, 3 * GROUP)
    wg = jnp.pad(wg, ((0, 0), (0, 0), (0, 0), (0, HEAD_DIM - 3 * GROUP)))
    return wg.reshape(nl, d, N_KV * HEAD_DIM)


def _forward(x, mod5, kvmod5, hist, paged, p, tiles):
    bsz, t, d = x.shape
    bb, tt = tiles["bb"], tiles["tt"]
    n_a = p["conv_w_in"].shape[0]
    depth = p["ffn_w_in"].shape[0]
    norm4 = p["norm_g"].reshape(depth, 3, 1, d)
    new_hist = []
    kv_all = None
    cmp_kv = None
    for l in range(depth):
        if l == n_a:
            kv_all = _kv_proj(x, kvmod5, p["kv_norm_g"], p["w_kv"], p["k_norm_g"], bb, tt)
            if paged is None:
                cmp_kv = _compress_prompt(kv_all, p["pos_t"], p["cmp_w1"], p["cmp_b1"], p["cmp_w2"], p["kgain0"])
            else:
                cmp_kv = _compress_paged(paged["cache4"], paged["page_table"], p["pos_t"], p["cmp_w1"],
                                         p["cmp_b1"], p["cmp_w2"], p["kgain0"], tiles["cmp_pages"])
        x = _ffn(x, mod5, norm4, p["ffn_w_in"], p["ffn_w_out"], l, 0, 0, 0, bb, tt, tiles["tf"], tiles["tn"])
        if l < n_a:
            u = _glu(x, mod5, norm4, p["conv_w_in"], p["conv_b_in"], l, bb, tt, tiles["tn"])
            hist_pad = jnp.pad(hist[l], ((0, 0), (HALO - (CONV_W - 1), 0), (0, 0)))
            x = _conv_tail(u, hist_pad, x, mod5, p["conv_dw"], p["conv_dw_b"], p["conv_ln_g"], p["conv_ln_b"],
                           p["conv_w_out"], p["conv_b_out"], l, bb, tt, tiles["tn"])
            new_hist.append(jnp.concatenate([hist[l], u], axis=1)[:, -(CONV_W - 1):])
        else:
            lj = l - n_a
            q, gates = _q_proj(x, mod5, norm4, p["w_q"], p["w_gate"], p["q_gain"], l, lj, bb, tt)
            if paged is None:
                a = _nsa_prompt(q, gates, cmp_kv, kv_all, p["slopes"], tiles["tq"], tiles["tk"])
            else:
                a = _nsa_paged(q, gates, cmp_kv, kv_all, paged["cache4"], paged["cache_win3"],
                               paged["page_table"], p["slopes"], tiles["nsa_pages"])
            x = _o_proj(a, x, mod5, p["w_o"], l, lj, bb, tt, tiles["tn"])
        x = _ffn(x, mod5, norm4, p["ffn_w_in"], p["ffn_w_out"], l, 1, 6, 2, bb, tt, tiles["tf"], tiles["tn"])
    return x, kv_all, jnp.stack(new_hist)


def kernel(x_prompt, x_sample, c_prompt, c_sample, cache_kv, cache_win, state_conv, page_table, ada_w, ada_b, norm_g, ffn_w_in, ffn_w_out, conv_w_in, conv_b_in, conv_dw, conv_dw_b, conv_ln_g, conv_ln_b, conv_w_out, conv_b_out, kv_norm_g, kv_ada_w, kv_ada_b, w_kv, cmp_pos, cmp_w1, cmp_b1, cmp_w2, k_norm_g, w_qg, q_norm_g, w_o):
    bp, tp, d = x_prompt.shape
    bs, ts, _ = x_sample.shape
    depth = ada_w.shape[0]
    hd = N_KV * GROUP * HEAD_DIM
    n_b = w_qg.shape[0]

    n_c = bp + bs
    c_all = jnp.pad(jnp.concatenate([c_prompt, c_sample], axis=0), ((0, -n_c % 16), (0, 0)))
    mod = _ada(c_all, ada_w, ada_b, 1024).reshape(depth, c_all.shape[0], 9, 1, d)
    kvmod = _ada(c_all, kv_ada_w[None], kv_ada_b[None], 1024).reshape(1, c_all.shape[0], 2, 1, d)

    p = dict(norm_g=norm_g, ffn_w_in=ffn_w_in.astype(BF), ffn_w_out=ffn_w_out.astype(BF),
             conv_w_in=conv_w_in.astype(BF), conv_b_in=conv_b_in, conv_dw=conv_dw, conv_dw_b=conv_dw_b,
             conv_ln_g=conv_ln_g, conv_ln_b=conv_ln_b, conv_w_out=conv_w_out.astype(BF), conv_b_out=conv_b_out,
             kv_norm_g=kv_norm_g, w_kv=w_kv.astype(BF), k_norm_g=k_norm_g,
             cmp_w1=cmp_w1.astype(BF), cmp_b1=cmp_b1, cmp_w2=cmp_w2,
             pos_t=jnp.transpose(cmp_pos, (1, 0, 2)),
             kgain0=k_norm_g[0].reshape(1, HEAD_DIM),
             w_q=w_qg[:, :, :hd].astype(BF), w_gate=_gate_weights(w_qg, hd).astype(BF),
             q_gain=jnp.tile(q_norm_g, (1, GROUP)).reshape(n_b, 1, GROUP * HEAD_DIM),
             w_o=w_o.astype(BF), slopes=_alibi_slopes())

    hist0 = jnp.zeros((conv_dw.shape[0], bp, CONV_W - 1, d), F32)
    tiles_p = dict(bb=1, tt=512, tf=512, tn=512, tq=128, tk=512)
    y_prompt, kv_p, conv_prompt = _forward(x_prompt, mod[:, :bp], kvmod[:, :bp], hist0, None, p, tiles_p)
    kv_prompt = kv_p[:, :, :4 * KV_SLOT].reshape(bp, tp, 4, N_KV, HEAD_DIM)
    wk = min(WINDOW, tp)
    win_prompt = kv_p[:, tp - wk:, 4 * KV_SLOT:].reshape(bp, wk, 2, N_KV, HEAD_DIM)

    n_pool, page = cache_kv.shape[0], cache_kv.shape[1]
    wlen = cache_win.shape[1]
    n_pages = page_table.shape[1]
    paged = dict(cache4=cache_kv.reshape(n_pool, page, 4 * N_KV, HEAD_DIM),
                 cache_win3=cache_win.reshape(bs, wlen, 2 * N_KV, HEAD_DIM),
                 page_table=page_table)
    tiles_s = dict(bb=bs, tt=ts, tf=512, tn=512, cmp_pages=min(32, n_pages), nsa_pages=min(16, n_pages))
    y_sample, kv_s, conv_sample = _forward(x_sample, mod[:, bp:n_c], kvmod[:, bp:n_c], state_conv, paged, p, tiles_s)
    kv_sample = kv_s[:, :, :4 * KV_SLOT].reshape(bs, ts, 4, N_KV, HEAD_DIM)
    win_new = kv_s[:, :, 4 * KV_SLOT:].reshape(bs, ts, 2, N_KV, HEAD_DIM)
    win_sample = jnp.concatenate([cache_win, win_new], axis=1)[:, -wlen:]
    return (y_prompt, y_sample, kv_prompt, kv_sample, win_prompt, win_sample, conv_prompt, conv_sample)
```

```python
import functools
import math

import numpy as np
import jax
import jax.numpy as jnp
from jax import lax
from jax.experimental import pallas as pl
from jax.experimental.pallas import tpu as pltpu

HEAD_DIM = 128
N_KV = 4
GROUP = 4
BLK = 64
TOP_N = 16
WINDOW = 512
CONV_W = 31
HALO = 32
SUBLANES = 8
EPS = 1e-6
NEG = -1e30
BIG = 1e30
M_FLOOR = -1e29
FORCE = 1e4
HALF = 0.5
SCALE = HEAD_DIM ** -0.5
LOG2E = math.log2(math.e)
KV_SLOT = N_KV * HEAD_DIM

BF = jnp.bfloat16
F32 = jnp.float32

V7X_VMEM_BYTES = 64 * 1024 * 1024
VMEM_LIMIT = V7X_VMEM_BYTES - 8 * 1024 * 1024


def _largest_tile(n, cap):
    return max(c for c in range(128, min(n, cap) + 1, 128) if n % c == 0)


def _params(*sem):
    return pltpu.CompilerParams(dimension_semantics=sem, vmem_limit_bytes=VMEM_LIMIT)


def _dot(a, b):
    return jnp.dot(a, b, preferred_element_type=F32)


def _dot_nt(a, b):
    return lax.dot_general(a, b, (((1,), (1,)), ((), ())), preferred_element_type=F32)


def _dot_tn(a, b):
    return lax.dot_general(a, b, (((0,), (0,)), ((), ())), preferred_element_type=F32)


def _sigmoid(x):
    return jax.nn.sigmoid(x)


def _rms_groups(y, gain):
    outs = []
    for c in range(y.shape[-1] // HEAD_DIM):
        ch = y[:, c * HEAD_DIM:(c + 1) * HEAD_DIM]
        outs.append(ch * lax.rsqrt(jnp.mean(ch * ch, axis=-1, keepdims=True) + EPS))
    return jnp.concatenate(outs, axis=-1) * gain


def _modulate_into(h_ref, x_ref, g_ref, sh_ref, sc_ref):
    x = x_ref[...]
    bb, tt, d = x.shape
    y = x * lax.rsqrt(jnp.mean(x * x, axis=-1, keepdims=True) + EPS) * g_ref[...]
    h = y * (1.0 + sc_ref[...]) + sh_ref[...]
    h_ref[...] = h.reshape(bb * tt, d).astype(BF)


def _ada_kernel(c_ref, w_ref, b_ref, o_ref):
    c = c_ref[...]
    sc = (c * _sigmoid(c)).astype(BF)
    o_ref[...] = _dot(sc, w_ref[...].astype(BF)) + b_ref[...]


def _ada(c, w, b, tn):
    nl, k, n = w.shape
    m = c.shape[0]
    return pl.pallas_call(
        _ada_kernel,
        out_shape=jax.ShapeDtypeStruct((nl, m, n), F32),
        grid=(nl, n // tn),
        in_specs=[pl.BlockSpec((m, k), lambda l, j: (0, 0)),
                  pl.BlockSpec((None, k, tn), lambda l, j: (l, 0, j)),
                  pl.BlockSpec((None, 1, tn), lambda l, j: (l, 0, j))],
        out_specs=pl.BlockSpec((None, m, tn), lambda l, j: (l, 0, j)),
        compiler_params=_params("parallel", "parallel"),
        name="ada_proj",
    )(c, w, b.reshape(nl, 1, n))


def _x_spec(bb, tt, d):
    return pl.BlockSpec((bb, tt, d), lambda b, i, j: (b, i, 0))


def _mod_spec(bb, d, l, idx):
    return pl.BlockSpec((None, bb, None, 1, d), lambda b, i, j: (l, b, idx, 0, 0))


def _gain_spec(d, l, k):
    return pl.BlockSpec((None, None, 1, d), lambda b, i, j: (l, k, 0, 0))


def _ffn_up_kernel(x_ref, sh_ref, sc_ref, g_ref, wa_ref, wb_ref, o_ref, h_ref):
    @pl.when(pl.program_id(2) == 0)
    def _():
        _modulate_into(h_ref, x_ref, g_ref, sh_ref, sc_ref)

    h = h_ref[...]
    a = _dot(h, wa_ref[...])
    b = _dot(h, wb_ref[...])
    o_ref[...] = (a * _sigmoid(a) * b).astype(BF)


def _ffn_down_kernel(a_ref, x_ref, gt_ref, w_ref, o_ref):
    bb, tt, n = x_ref.shape
    out = _dot(a_ref[...], w_ref[...])
    o_ref[...] = x_ref[...] + HALF * gt_ref[...] * out.reshape(bb, tt, n)


def _ffn(x, mod5, norm4, w_in, w_out, l, sub, mod_base, norm_idx, bb, tt, tf, tn):
    bsz, t, d = x.shape
    f = w_out.shape[2]
    nf = f // tf
    nt = t // tt
    act = pl.pallas_call(
        _ffn_up_kernel,
        out_shape=jax.ShapeDtypeStruct((bsz * t, f), BF),
        grid=(bsz // bb, nt, nf),
        in_specs=[_x_spec(bb, tt, d),
                  _mod_spec(bb, d, l, mod_base), _mod_spec(bb, d, l, mod_base + 1),
                  _gain_spec(d, l, norm_idx),
                  pl.BlockSpec((None, None, d, tf), lambda b, i, j: (l, sub, 0, j)),
                  pl.BlockSpec((None, None, d, tf), lambda b, i, j: (l, sub, 0, nf + j))],
        out_specs=pl.BlockSpec((bb * tt, tf), lambda b, i, j: (b * nt + i, j)),
        scratch_shapes=[pltpu.VMEM((bb * tt, d), BF)],
        compiler_params=_params("parallel", "parallel", "arbitrary"),
        name="ffn_up",
    )(x, mod5, mod5, norm4, w_in, w_in)
    return pl.pallas_call(
        _ffn_down_kernel,
        out_shape=jax.ShapeDtypeStruct(x.shape, F32),
        grid=(bsz // bb, nt, d // tn),
        in_specs=[pl.BlockSpec((bb * tt, f), lambda b, i, j: (b * nt + i, 0)),
                  pl.BlockSpec((bb, tt, tn), lambda b, i, j: (b, i, j)),
                  pl.BlockSpec((None, bb, None, 1, tn), lambda b, i, j: (l, b, mod_base + 2, 0, j)),
                  pl.BlockSpec((None, None, f, tn), lambda b, i, j: (l, sub, 0, j))],
        out_specs=pl.BlockSpec((bb, tt, tn), lambda b, i, j: (b, i, j)),
        compiler_params=_params("parallel", "parallel", "arbitrary"),
        name="ffn_down",
    )(act, x, mod5, w_out)


def _glu_kernel(x_ref, sh_ref, sc_ref, g_ref, wa_ref, wg_ref, ba_ref, bg_ref, o_ref, h_ref):
    j = pl.program_id(2)
    bb, tt, _ = x_ref.shape

    @pl.when(j == 0)
    def _():
        _modulate_into(h_ref, x_ref, g_ref, sh_ref, sc_ref)

    h = h_ref[...]
    a = _dot(h, wa_ref[...]) + ba_ref[...]
    g = _dot(h, wg_ref[...]) + bg_ref[...]
    o_ref[...] = (a * _sigmoid(g)).reshape(bb, tt, a.shape[-1])


def _glu(x, mod5, norm4, w_in, b_in, l, bb, tt, tn):
    bsz, t, d = x.shape
    nn = d // tn
    b3 = b_in.reshape(b_in.shape[0], 1, 2 * d)
    return pl.pallas_call(
        _glu_kernel,
        out_shape=jax.ShapeDtypeStruct(x.shape, F32),
        grid=(bsz // bb, t // tt, nn),
        in_specs=[_x_spec(bb, tt, d),
                  _mod_spec(bb, d, l, 3), _mod_spec(bb, d, l, 4),
                  _gain_spec(d, l, 1),
                  pl.BlockSpec((None, d, tn), lambda b, i, j: (l, 0, j)),
                  pl.BlockSpec((None, d, tn), lambda b, i, j: (l, 0, nn + j)),
                  pl.BlockSpec((None, 1, tn), lambda b, i, j: (l, 0, j)),
                  pl.BlockSpec((None, 1, tn), lambda b, i, j: (l, 0, nn + j))],
        out_specs=pl.BlockSpec((bb, tt, tn), lambda b, i, j: (b, i, j)),
        scratch_shapes=[pltpu.VMEM((bb * tt, d), BF)],
        compiler_params=_params("parallel", "parallel", "arbitrary"),
        name="conv_glu",
    )(x, mod5, mod5, norm4, w_in, w_in, b3, b3)


def _conv_tail_kernel(u_ref, halo_ref, hist_ref, x_ref, gt_ref, dw_ref, dwb_ref, lng_ref, lnb_ref,
                      wo_ref, bo_ref, o_ref, ext_ref, z_ref, *, rows_per_step):
    i = pl.program_id(1)
    j = pl.program_id(2)
    bb, tt, d = u_ref.shape
    rps = rows_per_step

    @pl.when(j == 0)
    def _():
        ext_ref[:, 0:HALO, :] = jnp.where(i == 0, hist_ref[...], halo_ref[...])
        ext_ref[:, HALO:HALO + tt, :] = u_ref[...]
        off = HALO - (CONV_W - 1)

        def rows(c, carry):
            r0 = pl.multiple_of(c * rps, rps)
            parts = []
            for c0 in range(0, d, HEAD_DIM):
                lanes = slice(c0, c0 + HEAD_DIM)
                win = ext_ref[:, pl.ds(r0, rps + HALO), lanes]
                part = jnp.zeros((bb, rps, HEAD_DIM), F32) + dwb_ref[:, lanes]
                for s in range(SUBLANES):
                    taps = [w for w in range(CONV_W) if (off + w) % SUBLANES == s]
                    if s == 0:
                        shifted = win
                    else:
                        shifted = jnp.stack([pltpu.roll(win[bi], rps + HALO - s, axis=0) for bi in range(bb)])
                    for w in taps:
                        a0 = (off + w) // SUBLANES * SUBLANES
                        part = part + dw_ref[w:w + 1, lanes] * shifted[:, a0:a0 + rps, :]
                parts.append(part)
            acc = jnp.concatenate(parts, axis=-1)
            mu = jnp.mean(acc, axis=-1, keepdims=True)
            cen = acc - mu
            var = jnp.mean(cen * cen, axis=-1, keepdims=True)
            y = cen * lax.rsqrt(var + EPS) * lng_ref[...] + lnb_ref[...]
            z = y * _sigmoid(y)
            for bi in range(bb):
                z_ref[pl.ds(bi * tt + r0, rps), :] = z[bi]
            return carry

        lax.fori_loop(0, tt // rps, rows, 0)

    out = _dot(z_ref[...].astype(BF), wo_ref[...]) + bo_ref[...]
    o_ref[...] = x_ref[...] + gt_ref[...] * out.reshape(bb, tt, out.shape[-1])


def _conv_tail(u, hist_pad, x, mod5, dw, dw_b, ln_g, ln_b, w_out, b_out, l, bb, tt, tn):
    bsz, t, d = x.shape
    nl = dw.shape[0]
    if t > tt:
        per = tt // HALO
        halo_arr = u
        halo_spec = pl.BlockSpec((bb, HALO, d), lambda b, i, j: (b, jnp.maximum(i * per - 1, 0), 0))
    else:
        halo_arr = hist_pad
        halo_spec = pl.BlockSpec((bb, HALO, d), lambda b, i, j: (b, 0, 0))
    vec = lambda a: a.reshape(nl, 1, d)
    row_spec = pl.BlockSpec((None, 1, d), lambda b, i, j: (l, 0, 0))
    return pl.pallas_call(
        functools.partial(_conv_tail_kernel, rows_per_step=min(tt, 64)),
        out_shape=jax.ShapeDtypeStruct(x.shape, F32),
        grid=(bsz // bb, t // tt, d // tn),
        in_specs=[_x_spec(bb, tt, d),
                  halo_spec,
                  pl.BlockSpec((bb, HALO, d), lambda b, i, j: (b, 0, 0)),
                  pl.BlockSpec((bb, tt, tn), lambda b, i, j: (b, i, j)),
                  pl.BlockSpec((None, bb, None, 1, tn), lambda b, i, j: (l, b, 5, 0, j)),
                  pl.BlockSpec((None, CONV_W, d), lambda b, i, j: (l, 0, 0)),
                  row_spec, row_spec, row_spec,
                  pl.BlockSpec((None, d, tn), lambda b, i, j: (l, 0, j)),
                  pl.BlockSpec((None, 1, tn), lambda b, i, j: (l, 0, j))],
        out_specs=pl.BlockSpec((bb, tt, tn), lambda b, i, j: (b, i, j)),
        scratch_shapes=[pltpu.VMEM((bb, HALO + tt, d), F32), pltpu.VMEM((bb * tt, d), F32)],
        compiler_params=_params("parallel", "parallel", "arbitrary"),
        name="conv_tail",
    )(u, halo_arr, hist_pad, x, mod5, dw, vec(dw_b), vec(ln_g), vec(ln_b), w_out, vec(b_out))


def _kv_kernel(x_ref, sh_ref, sc_ref, g_ref, w_ref, kg_ref, o_ref, h_ref):
    j = pl.program_id(2)
    bb, tt, _ = x_ref.shape

    @pl.when(j == 0)
    def _():
        _modulate_into(h_ref, x_ref, g_ref, sh_ref, sc_ref)

    y = _dot(h_ref[...], w_ref[...])
    normed = _rms_groups(y, kg_ref[...])
    y = jnp.where((j == 2) | (j == 4), normed, y)
    o_ref[...] = y.reshape(bb, tt, y.shape[-1])


def _kv_proj(x, kvmod5, kv_norm_g, w_kv, k_norm_g, bb, tt):
    bsz, t, d = x.shape
    n = w_kv.shape[1]
    ns = n // KV_SLOT
    gains = jnp.ones((ns, 1, KV_SLOT), F32)
    gains = gains.at[2, 0].set(jnp.tile(k_norm_g[1], N_KV)).at[4, 0].set(jnp.tile(k_norm_g[2], N_KV))
    return pl.pallas_call(
        _kv_kernel,
        out_shape=jax.ShapeDtypeStruct((bsz, t, n), F32),
        grid=(bsz // bb, t // tt, ns),
        in_specs=[_x_spec(bb, tt, d),
                  _mod_spec(bb, d, 0, 0), _mod_spec(bb, d, 0, 1),
                  pl.BlockSpec((1, d), lambda b, i, j: (0, 0)),
                  pl.BlockSpec((d, KV_SLOT), lambda b, i, j: (0, j)),
                  pl.BlockSpec((None, 1, KV_SLOT), lambda b, i, j: (j, 0, 0))],
        out_specs=pl.BlockSpec((bb, tt, KV_SLOT), lambda b, i, j: (b, i, j)),
        scratch_shapes=[pltpu.VMEM((bb * tt, d), BF)],
        compiler_params=_params("parallel", "parallel", "arbitrary"),
        name="kv_proj",
    )(x, kvmod5, kvmod5, kv_norm_g.reshape(1, d), w_kv, gains)


def _q_kernel(x_ref, sh_ref, sc_ref, g_ref, wq_ref, wg_ref, qg_ref, q_ref, gate_ref, h_ref):
    j = pl.program_id(2)
    bb, tt, _ = x_ref.shape

    @pl.when(j == 0)
    def _():
        _modulate_into(h_ref, x_ref, g_ref, sh_ref, sc_ref)
        gl = _dot(h_ref[...], wg_ref[...])
        gate_ref[...] = _sigmoid(gl).reshape(bb, tt, gl.shape[-1])

    y = _dot(h_ref[...], wq_ref[...])
    q_ref[...] = _rms_groups(y, qg_ref[...]).reshape(bb, tt, y.shape[-1])


def _q_proj(x, mod5, norm4, w_q, w_gate, q_gain, l, lj, bb, tt):
    bsz, t, d = x.shape
    hd = w_q.shape[2]
    ng = w_gate.shape[2]
    return pl.pallas_call(
        _q_kernel,
        out_shape=(jax.ShapeDtypeStruct((bsz, t, hd), F32), jax.ShapeDtypeStruct((bsz, t, ng), F32)),
        grid=(bsz // bb, t // tt, hd // KV_SLOT),
        in_specs=[_x_spec(bb, tt, d),
                  _mod_spec(bb, d, l, 3), _mod_spec(bb, d, l, 4),
                  _gain_spec(d, l, 1),
                  pl.BlockSpec((None, d, KV_SLOT), lambda b, i, j: (lj, 0, j)),
                  pl.BlockSpec((None, d, ng), lambda b, i, j: (lj, 0, 0)),
                  pl.BlockSpec((None, 1, KV_SLOT), lambda b, i, j: (lj, 0, 0))],
        out_specs=(pl.BlockSpec((bb, tt, KV_SLOT), lambda b, i, j: (b, i, j)),
                   pl.BlockSpec((bb, tt, ng), lambda b, i, j: (b, i, 0))),
        scratch_shapes=[pltpu.VMEM((bb * tt, d), BF)],
        compiler_params=_params("parallel", "parallel", "arbitrary"),
        name="q_proj",
    )(x, mod5, mod5, norm4, w_q, w_gate, q_gain)


def _oproj_kernel(a_ref, x_ref, gt_ref, w_ref, o_ref):
    bb, tt, k = a_ref.shape
    out = _dot(a_ref[...].reshape(bb * tt, k).astype(BF), w_ref[...])
    o_ref[...] = x_ref[...] + gt_ref[...] * out.reshape(bb, tt, out.shape[-1])


def _o_proj(a, x, mod5, w_o, l, lj, bb, tt, tn):
    bsz, t, d = x.shape
    k = a.shape[2]
    return pl.pallas_call(
        _oproj_kernel,
        out_shape=jax.ShapeDtypeStruct(x.shape, F32),
        grid=(bsz // bb, t // tt, d // tn),
        in_specs=[_x_spec(bb, tt, k),
                  pl.BlockSpec((bb, tt, tn), lambda b, i, j: (b, i, j)),
                  pl.BlockSpec((None, bb, None, 1, tn), lambda b, i, j: (l, b, 5, 0, j)),
                  pl.BlockSpec((None, k, tn), lambda b, i, j: (lj, 0, j))],
        out_specs=pl.BlockSpec((bb, tt, tn), lambda b, i, j: (b, i, j)),
        compiler_params=_params("parallel", "parallel", "arbitrary"),
        name="o_proj",
    )(a, x, mod5, w_o)


def _compress_pair(rows_a, rows_b, j, pos_ref, w1_ref):
    lhs = jnp.concatenate(
        [jnp.concatenate([rows_a[g] + pos_ref[j:j + 1, :], rows_b[g] + pos_ref[j + 1:j + 2, :]], axis=-1)
         for g in range(N_KV)], axis=0)
    return _dot(lhs.astype(BF), w1_ref[j * HEAD_DIM:(j + 2) * HEAD_DIM, :])


def _compress_finish(acc, nb, b1_ref, w2_ref, kg_ref, is_key):
    hid = jax.nn.gelu(acc + b1_ref[...])
    out = _dot(hid.astype(BF), w2_ref[...].astype(BF))
    normed = out * lax.rsqrt(jnp.mean(out * out, axis=-1, keepdims=True) + EPS) * kg_ref[...]
    out = jnp.where(is_key, normed, out)
    return out.reshape(N_KV, nb, HEAD_DIM)


def _compress_prompt_kernel(r0_ref, r1_ref, r2_ref, r3_ref, pos_ref, w1_ref, b1_ref, w2_ref, kg_ref, o_ref):
    heads = [r0_ref, r1_ref, r2_ref, r3_ref]
    nb = r0_ref.shape[0] // BLK
    acc = jnp.zeros((N_KV * nb, w1_ref.shape[-1]), F32)
    for j in range(0, BLK, 2):
        acc = acc + _compress_pair([h[pl.ds(j, nb, stride=BLK), :] for h in heads],
                                   [h[pl.ds(j + 1, nb, stride=BLK), :] for h in heads], j, pos_ref, w1_ref)
    o_ref[...] = _compress_finish(acc, nb, b1_ref, w2_ref, kg_ref, pl.program_id(0) == 0)


def _compress_prompt(kv_all, pos_t, w1, b1, w2, kgain):
    bsz, t, _ = kv_all.shape
    nc = t // BLK
    kf = BLK * HEAD_DIM

    def head_spec(g):
        return pl.BlockSpec((None, nc * BLK, HEAD_DIM), lambda p, b: (b, 0, p * N_KV + g))

    return pl.pallas_call(
        _compress_prompt_kernel,
        out_shape=jax.ShapeDtypeStruct((2, bsz, N_KV, nc, HEAD_DIM), F32),
        grid=(2, bsz),
        in_specs=[head_spec(g) for g in range(N_KV)] + [
            pl.BlockSpec((None, BLK, HEAD_DIM), lambda p, b: (p, 0, 0)),
            pl.BlockSpec((None, kf, 2 * HEAD_DIM), lambda p, b: (p, 0, 0)),
            pl.BlockSpec((None, 1, 2 * HEAD_DIM), lambda p, b: (p, 0, 0)),
            pl.BlockSpec((None, 2 * HEAD_DIM, HEAD_DIM), lambda p, b: (p, 0, 0)),
            pl.BlockSpec((1, HEAD_DIM), lambda p, b: (0, 0))],
        out_specs=pl.BlockSpec((None, None, N_KV, nc, HEAD_DIM), lambda p, b: (p, b, 0, 0, 0)),
        compiler_params=_params("arbitrary", "arbitrary"),
        name="compress_prompt",
    )(kv_all, kv_all, kv_all, kv_all, pos_t, w1, b1.reshape(2, 1, -1), w2, kgain)


def _compress_paged_kernel(pt_ref, cache_ref, w1_hbm, pos_ref, b1_ref, w2_ref, kg_ref, o_ref,
                           rows_ref, w1_ref, sem, w1_sem):
    b = pl.program_id(0)
    s = pl.program_id(1)
    ns = pl.num_programs(1)
    step = b * ns + s
    n_steps = pl.num_programs(0) * ns
    page = cache_ref.shape[1]
    n_rows = rows_ref.shape[0] // 2
    n_pg = n_rows // page
    nb = n_rows // BLK
    n_sub = 2 * N_KV
    slot = step % 2
    base = pl.multiple_of(slot * n_rows, n_rows)

    def page_copy(bi, si, k, sl):
        src = cache_ref.at[pt_ref[bi, si * n_pg + k], :, pl.ds(0, n_sub), :]
        dst = rows_ref.at[pl.ds(pl.multiple_of(sl * n_rows + k * page, page), page), :, :]
        return pltpu.make_async_copy(src, dst, sem.at[sl])

    def start_step(bi, si, sl):
        def body(k, c):
            page_copy(bi, si, k, sl).start()
            return c
        lax.fori_loop(0, n_pg, body, 0)

    @pl.when(step == 0)
    def _():
        w1_copy = pltpu.make_async_copy(w1_hbm, w1_ref, w1_sem)
        w1_copy.start()
        start_step(b, s, slot)
        w1_copy.wait()

    @pl.when(step + 1 < n_steps)
    def _():
        nxt = step + 1
        start_step(nxt // ns, nxt % ns, 1 - slot)

    def wait_body(k, c):
        page_copy(b, s, k, slot).wait()
        return c

    lax.fori_loop(0, n_pg, wait_body, 0)

    def block_rows(j):
        return jnp.swapaxes(rows_ref[pl.ds(base + j, nb, stride=BLK), :, :], 0, 1)

    accs = [jnp.zeros((N_KV * nb, w1_ref.shape[-1]), F32) for _ in range(2)]
    for j in range(0, BLK, 2):
        xa = block_rows(j)
        xb = block_rows(j + 1)
        for p in range(2):
            heads = range(p * N_KV, (p + 1) * N_KV)
            accs[p] = accs[p] + _compress_pair([xa[h] for h in heads], [xb[h] for h in heads], j,
                                               pos_ref.at[p], w1_ref.at[p])
    for p in range(2):
        o_ref[p] = _compress_finish(accs[p], nb, b1_ref.at[p], w2_ref.at[p], kg_ref, p == 0)


def _compress_paged(cache4, page_table, pos_t, w1, b1, w2, kgain, pages_per_step):
    bsz, n_pages = page_table.shape
    page = cache4.shape[1]
    nb = pages_per_step * page // BLK
    nc = n_pages * page // BLK
    full = lambda a: pl.BlockSpec(a.shape, lambda b, s, pt: (0,) * a.ndim)
    b1r = b1.reshape(2, 1, -1)
    grid_spec = pltpu.PrefetchScalarGridSpec(
        num_scalar_prefetch=1,
        grid=(bsz, n_pages // pages_per_step),
        in_specs=[pl.BlockSpec(memory_space=pl.ANY), pl.BlockSpec(memory_space=pl.ANY),
                  full(pos_t), full(b1r), full(w2), full(kgain)],
        out_specs=pl.BlockSpec((2, None, N_KV, nb, HEAD_DIM), lambda b, s, pt: (0, b, 0, s, 0)),
        scratch_shapes=[pltpu.VMEM((2 * pages_per_step * page, 2 * N_KV, HEAD_DIM), F32),
                        pltpu.VMEM(w1.shape, w1.dtype),
                        pltpu.SemaphoreType.DMA((2,)), pltpu.SemaphoreType.DMA],
    )
    return pl.pallas_call(
        _compress_paged_kernel,
        out_shape=jax.ShapeDtypeStruct((2, bsz, N_KV, nc, HEAD_DIM), F32),
        grid_spec=grid_spec,
        compiler_params=_params("arbitrary", "arbitrary"),
        name="compress_paged",
    )(page_table, cache4, w1, pos_t, b1r, w2, kgain)


def _stack_heads(q_val):
    return jnp.concatenate([q_val[:, r * HEAD_DIM:(r + 1) * HEAD_DIM] for r in range(GROUP)], axis=0)


def _masked_scores(q, k_bf, valid, dist, slopes_ref, g):
    t = dist.shape[0]
    raw = _dot_nt(q, k_bf)
    return jnp.concatenate(
        [jnp.where(valid, raw[r * t:(r + 1) * t] * SCALE - slopes_ref[g, r] * dist, NEG) for r in range(GROUP)],
        axis=0)


def _softmax_full(s):
    m = jnp.maximum(jnp.max(s, axis=-1, keepdims=True), M_FLOOR)
    e = jnp.exp(s - m)
    return e, jnp.sum(e, axis=-1, keepdims=True)


def _online_update(s, v_bf, m_old, l_old, acc_old):
    m_new = jnp.maximum(m_old, jnp.max(s, axis=-1, keepdims=True))
    e = jnp.exp(s - jnp.maximum(m_new, M_FLOOR))
    alpha = jnp.exp(m_old - m_new)
    l_new = alpha * l_old + jnp.sum(e, axis=-1, keepdims=True)
    acc_new = alpha * acc_old + _dot(e.astype(BF), v_bf)
    return m_new, l_new, acc_new


def _head_sum(p, t):
    out = p[0:t]
    for r in range(1, GROUP):
        out = out + p[r * t:(r + 1) * t]
    return out


def _top_blocks(score, blk, k):
    n = score.shape[-1]
    sel = jnp.zeros(score.shape, F32)
    for _ in range(k):
        mx = jnp.max(score, axis=-1, keepdims=True)
        idx = jnp.min(jnp.where(score == mx, blk, float(n)), axis=-1, keepdims=True)
        hit = blk == idx
        sel = jnp.where(hit & (mx >= 0.0), 1.0, sel)
        score = jnp.where(hit, -2.0, score)
    return sel


def _rank_select_t(score_t, k):
    n, t = score_t.shape
    groups = [score_t[r0:r0 + SUBLANES] for r0 in range(0, n, SUBLANES)]
    ranks = [jnp.zeros((SUBLANES, t), F32) for _ in groups]
    sub = lax.broadcasted_iota(jnp.int32, (SUBLANES, t), 0)
    for j in range(n):
        sj = score_t[j:j + 1, :]
        for gi, grp in enumerate(groups):
            lo = gi * SUBLANES
            ge = jnp.where(sj >= grp, 1.0, 0.0)
            gt = jnp.where(sj > grp, 1.0, 0.0)
            if lo > j:
                beats = ge
            elif lo + SUBLANES - 1 <= j:
                beats = gt
            else:
                beats = jnp.where(sub > j - lo, ge, gt)
            ranks[gi] = ranks[gi] + beats
    rank = jnp.concatenate(ranks, axis=0)
    return jnp.where((rank < float(k)) & (score_t >= 0.0), 1.0, 0.0)


def _expand_matrix(nsel, first_blk, n_keys):
    row = lax.broadcasted_iota(jnp.int32, (nsel, n_keys), 0)
    col = lax.broadcasted_iota(jnp.int32, (nsel, n_keys), 1)
    return jnp.where(row == first_blk + col // BLK, 1.0, 0.0).astype(BF)


def _combine(gate, o_cmp, o_sel, o_win, t):
    cols = []
    for r in range(GROUP):
        rs = slice(r * t, (r + 1) * t)
        cols.append(gate[:, r:r + 1] * o_cmp[rs] + gate[:, GROUP + r:GROUP + r + 1] * o_sel[rs]
                    + gate[:, 2 * GROUP + r:2 * GROUP + r + 1] * o_win[rs])
    return jnp.concatenate(cols, axis=-1)


def _nsa_prompt_kernel(slopes_ref, q_ref, gate_ref, kc_ref, vc_ref, ks_ref, vs_ref, kw_ref, vw_ref, relb_ref,
                       o_ref, m_ref, l_ref, acc_ref, any_ref, *, tk):
    g = pl.program_id(1)
    qi = pl.program_id(2)
    tq = q_ref.shape[0]
    nc = kc_ref.shape[0]
    rows = GROUP * tq
    t0 = qi * tq
    q = _stack_heads(q_ref[...]).astype(BF)
    c1 = SCALE * LOG2E

    blk_t = lax.broadcasted_iota(jnp.int32, (nc, tq), 0)
    tok_t = t0 + lax.broadcasted_iota(jnp.int32, (nc, tq), 1)
    dist_t = (tok_t - (blk_t * BLK + (BLK - 1))).astype(F32)
    valid_t = dist_t >= 0
    raw_t = _dot_nt(kc_ref[...].astype(BF), q)
    p_heads = []
    for r in range(GROUP):
        s = jnp.where(valid_t, raw_t[:, r * tq:(r + 1) * tq] * SCALE - slopes_ref[g, r] * dist_t, NEG)
        m = jnp.maximum(jnp.max(s, axis=0, keepdims=True), M_FLOOR)
        e = jnp.exp(s - m)
        p_heads.append(e / jnp.maximum(jnp.sum(e, axis=0, keepdims=True), 1e-30))
    o_cmp = _dot_tn(jnp.concatenate(p_heads, axis=1).astype(BF), vc_ref[...].astype(BF))
    imp_t = p_heads[0]
    for r in range(1, GROUP):
        imp_t = imp_t + p_heads[r]

    cur_t = tok_t // BLK
    forced = (blk_t == 0) | (blk_t == cur_t) | (blk_t == cur_t - 1)
    score_t = jnp.where(forced, FORCE, imp_t)
    score_t = jnp.where(blk_t <= cur_t, score_t, -1.0)
    sel_t = _rank_select_t(score_t, min(TOP_N, nc))
    unsel = ((1.0 - sel_t) * BIG).T.astype(BF)

    m_ref[...] = jnp.full(m_ref.shape, NEG, F32)
    l_ref[...] = jnp.zeros(l_ref.shape, F32)
    acc_ref[...] = jnp.zeros(acc_ref.shape, F32)
    rel = lax.broadcasted_iota(jnp.int32, (tq, tk), 0) - lax.broadcasted_iota(jnp.int32, (tq, tk), 1)
    slope_col = jnp.concatenate([jnp.full((tq, 1), slopes_ref[g, r] * LOG2E, F32) for r in range(GROUP)], axis=0)

    bpt = tk // BLK
    blk_any = jnp.max(sel_t, axis=1, keepdims=True)
    for kt in range(nc // bpt):
        any_ref[kt] = jnp.max(blk_any[kt * bpt:(kt + 1) * bpt])

    def key_tile(kt, causal):
        k0 = pl.multiple_of(kt * tk, tk)
        raw = _dot_nt(q, ks_ref[pl.ds(k0, tk), :].astype(BF))
        mask = _dot(unsel, _expand_matrix(nc, kt * bpt, tk))
        if causal:
            mask = mask + jnp.where(rel + (t0 - k0) >= 0, 0.0, BIG)
        s = jnp.concatenate([raw[r * tq:(r + 1) * tq] * c1 - relb_ref[r * tq:(r + 1) * tq, 0:tk] - mask
                             for r in range(GROUP)], axis=0)
        col = slope_col * (t0 - k0).astype(F32)
        m_old = m_ref[...]
        m_new = jnp.maximum(m_old, jnp.max(s, axis=-1, keepdims=True) - col)
        e = jnp.exp2(s - (jnp.maximum(m_new, M_FLOOR) + col))
        alpha = jnp.exp2(m_old - m_new)
        l_ref[...] = alpha * l_ref[...] + jnp.sum(e, axis=-1, keepdims=True)
        acc_ref[...] = alpha * acc_ref[...] + _dot(e.astype(BF), vs_ref[pl.ds(k0, tk), :].astype(BF))
        m_ref[...] = m_new

    k_diag = t0 // tk

    def early_tile(kt, carry):
        @pl.when(any_ref[kt] > 0.5)
        def _():
            key_tile(kt, causal=False)
        return carry

    lax.fori_loop(0, k_diag, early_tile, 0)
    key_tile(k_diag, causal=True)
    o_sel = acc_ref[...] / jnp.maximum(l_ref[...], 1e-30)

    lw = WINDOW + tq
    w0 = pl.multiple_of(jnp.maximum(t0 - WINDOW, 0), SUBLANES)
    dist = (t0 - w0) + (lax.broadcasted_iota(jnp.int32, (tq, lw), 0) - lax.broadcasted_iota(jnp.int32, (tq, lw), 1))
    mask = jnp.where((dist >= 0) & (dist < WINDOW), 0.0, BIG)
    raw = _dot_nt(q, kw_ref[pl.ds(w0, lw), :].astype(BF))
    s = jnp.concatenate([raw[r * tq:(r + 1) * tq] * c1 - relb_ref[r * tq:(r + 1) * tq, 0:lw] - mask
                         for r in range(GROUP)], axis=0)
    m = jnp.maximum(jnp.max(s, axis=-1, keepdims=True), M_FLOOR)
    e = jnp.exp2(s - m)
    o_win = _dot(e.astype(BF), vw_ref[pl.ds(w0, lw), :].astype(BF)) / jnp.maximum(
        jnp.sum(e, axis=-1, keepdims=True), 1e-30)

    o_ref[...] = _combine(gate_ref[...], o_cmp, o_sel, o_win, tq)


def _nsa_prompt(q, gates, cmp_kv, kv_all, slopes, tq, tk):
    bsz, t, hd = q.shape
    nc = cmp_kv.shape[3]
    lw = WINDOW + tq
    assert t % BLK == 0 and t % tk == 0 and t >= lw and tk <= lw and tk % tq == 0 and tk % BLK == 0
    rows = GROUP * tq
    rel = (jnp.arange(tq, dtype=F32)[:, None] - jnp.arange(lw, dtype=F32)[None, :])
    relb = ((slopes * LOG2E)[:, :, None, None] * rel[None, None]).reshape(N_KV, rows, lw)

    def col(slot):
        return pl.BlockSpec((None, t, HEAD_DIM), lambda b, g, i: (b, 0, slot * N_KV + g))

    def cmp_spec(p):
        return pl.BlockSpec((None, None, None, nc, HEAD_DIM), lambda b, g, i: (p, b, g, 0, 0))

    return pl.pallas_call(
        functools.partial(_nsa_prompt_kernel, tk=tk),
        out_shape=jax.ShapeDtypeStruct(q.shape, F32),
        grid=(bsz, N_KV, t // tq),
        in_specs=[pl.BlockSpec(memory_space=pltpu.SMEM),
                  pl.BlockSpec((None, tq, GROUP * HEAD_DIM), lambda b, g, i: (b, i, g)),
                  pl.BlockSpec((None, tq, HEAD_DIM), lambda b, g, i: (b, i, g)),
                  cmp_spec(0), cmp_spec(1), col(2), col(3), col(4), col(5),
                  pl.BlockSpec((None, rows, lw), lambda b, g, i: (g, 0, 0))],
        out_specs=pl.BlockSpec((None, tq, GROUP * HEAD_DIM), lambda b, g, i: (b, i, g)),
        scratch_shapes=[pltpu.VMEM((rows, 1), F32), pltpu.VMEM((rows, 1), F32),
                        pltpu.VMEM((rows, HEAD_DIM), F32), pltpu.SMEM((t // tk,), F32)],
        compiler_params=_params("parallel", "parallel", "arbitrary"),
        name="nsa_prompt",
    )(slopes, q, gates, cmp_kv, cmp_kv, kv_all, kv_all, kv_all, kv_all, relb)


def _nsa_paged_kernel(pt_ref, slopes_ref, q_ref, gate_ref, kc_ref, vc_ref, new_ref, cwin_ref, *rest,
                      n_pg, past_len):
    page_refs = rest[:n_pg]
    o_ref, sel_ref, m_ref, l_ref, acc_ref, ocmp_ref, owin_ref, pad_ref = rest[n_pg:]
    s_idx = pl.program_id(1)
    tq = q_ref.shape[0]
    nc = kc_ref.shape[1]
    page = page_refs[0].shape[0]
    rows = GROUP * tq
    npad = pad_ref.shape[0]
    wlen = cwin_ref.shape[0]

    def q_rows(g):
        return _stack_heads(q_ref[:, g * KV_SLOT:(g + 1) * KV_SLOT]).astype(BF)

    def tok_pos(n):
        return past_len + lax.broadcasted_iota(jnp.int32, (tq, n), 0)

    def new_cols(slot, g):
        return pad_ref[:, slot * KV_SLOT + g * HEAD_DIM:slot * KV_SLOT + (g + 1) * HEAD_DIM]

    def by_head(ref):
        return jnp.swapaxes(ref[...], 0, 1)

    @pl.when(s_idx == 0)
    def _():
        pad_ref[...] = jnp.zeros(pad_ref.shape, F32)
        pad_ref[0:tq, :] = new_ref[...]
        new_i = lax.broadcasted_iota(jnp.int32, (tq, npad), 1)
        blk_i = lax.broadcasted_iota(jnp.int32, (tq, nc), 1)
        cwin = by_head(cwin_ref)
        imps = []
        for g in range(N_KV):
            q = q_rows(g)
            dist = (tok_pos(nc) - (blk_i * BLK + (BLK - 1))).astype(F32)
            e, l = _softmax_full(_masked_scores(q, kc_ref[g].astype(BF), dist >= 0, dist, slopes_ref, g))
            p_cmp = e / jnp.maximum(l, 1e-30)
            ocmp_ref[g] = _dot(p_cmp.astype(BF), vc_ref[g].astype(BF))
            imps.append(_head_sum(p_cmp, tq))
            dist = (tok_pos(npad) - (past_len + new_i)).astype(F32)
            valid = (dist >= 0) & (new_i < tq)
            s = _masked_scores(q, new_cols(2, g).astype(BF), valid, dist, slopes_ref, g)
            m, l, acc = _online_update(s, new_cols(3, g).astype(BF), jnp.full((rows, 1), NEG, F32),
                                       jnp.zeros((rows, 1), F32), jnp.zeros((rows, HEAD_DIM), F32))
            m_ref[g] = m
            l_ref[g] = l
            acc_ref[g] = acc
            kw = jnp.concatenate([cwin[g], new_cols(4, g)], axis=0)
            vw = jnp.concatenate([cwin[N_KV + g], new_cols(5, g)], axis=0)
            win_i = lax.broadcasted_iota(jnp.int32, (tq, wlen + npad), 1)
            dist = (tok_pos(wlen + npad) - (past_len - wlen + win_i)).astype(F32)
            valid = (dist >= 0) & (dist < WINDOW) & (win_i < wlen + tq)
            e, l = _softmax_full(_masked_scores(q, kw.astype(BF), valid, dist, slopes_ref, g))
            owin_ref[g] = _dot(e.astype(BF), vw.astype(BF)) / jnp.maximum(l, 1e-30)
        blk_all = lax.broadcasted_iota(jnp.int32, (N_KV * tq, nc), 1)
        score = jnp.where((blk_all == 0) | (blk_all == nc - 1), FORCE, jnp.concatenate(imps, axis=0))
        sel = _top_blocks(score, blk_all.astype(F32), min(TOP_N, nc + 1) - 1)
        for g in range(N_KV):
            sel_ref[g] = sel[g * tq:(g + 1) * tq]

    nk = n_pg * page
    k0 = s_idx * nk
    kpos = k0 + lax.broadcasted_iota(jnp.int32, (tq, nk), 1)
    dist = (tok_pos(nk) - kpos).astype(F32)
    expand = _expand_matrix(nc, s_idx * (nk // BLK), nk)
    pages = [by_head(pr) for pr in page_refs]
    for g in range(N_KV):
        q = q_rows(g)
        ks = jnp.concatenate([pg[g] for pg in pages], axis=0)
        vs = jnp.concatenate([pg[N_KV + g] for pg in pages], axis=0)
        valid = (_dot(sel_ref[g].astype(BF), expand) > 0.5) & (dist >= 0)
        s = _masked_scores(q, ks.astype(BF), valid, dist, slopes_ref, g)
        m, l, acc = _online_update(s, vs.astype(BF), m_ref[g], l_ref[g], acc_ref[g])
        m_ref[g] = m
        l_ref[g] = l
        acc_ref[g] = acc

    @pl.when(s_idx == pl.num_programs(1) - 1)
    def _():
        outs = []
        for g in range(N_KV):
            o_sel = acc_ref[g] / jnp.maximum(l_ref[g], 1e-30)
            outs.append(_combine(gate_ref[:, g * HEAD_DIM:(g + 1) * HEAD_DIM], ocmp_ref[g], o_sel, owin_ref[g], tq))
        o_ref[...] = jnp.concatenate(outs, axis=-1)


def _nsa_paged(q, gates, cmp_kv, kv_new, cache4, cache_win3, page_table, slopes, n_pg):
    bsz, tq, hd = q.shape
    n_pages = page_table.shape[1]
    page = cache4.shape[1]
    past_len = n_pages * page
    nc = cmp_kv.shape[3]
    wlen = cache_win3.shape[1]
    assert past_len % BLK == 0 and tq < BLK and nc == past_len // BLK and n_pages % n_pg == 0
    assert wlen == WINDOW and tq % SUBLANES == 0
    rows = GROUP * tq
    npad = 128

    def cmp_spec(p):
        return pl.BlockSpec((None, None, N_KV, nc, HEAD_DIM), lambda b, s, pt: (p, b, 0, 0, 0))

    def page_spec(k):
        return pl.BlockSpec((None, page, 2 * N_KV, HEAD_DIM), lambda b, s, pt: (pt[b, s * n_pg + k], 0, 1, 0))

    grid_spec = pltpu.PrefetchScalarGridSpec(
        num_scalar_prefetch=1,
        grid=(bsz, n_pages // n_pg),
        in_specs=[pl.BlockSpec(memory_space=pltpu.SMEM),
                  pl.BlockSpec((None, tq, hd), lambda b, s, pt: (b, 0, 0)),
                  pl.BlockSpec((None, tq, gates.shape[2]), lambda b, s, pt: (b, 0, 0)),
                  cmp_spec(0), cmp_spec(1),
                  pl.BlockSpec((None, tq, kv_new.shape[2]), lambda b, s, pt: (b, 0, 0)),
                  pl.BlockSpec((None, wlen, 2 * N_KV, HEAD_DIM), lambda b, s, pt: (b, 0, 0, 0))]
                 + [page_spec(k) for k in range(n_pg)],
        out_specs=pl.BlockSpec((None, tq, hd), lambda b, s, pt: (b, 0, 0)),
        scratch_shapes=[pltpu.VMEM((N_KV, tq, nc), F32),
                        pltpu.VMEM((N_KV, rows, 1), F32), pltpu.VMEM((N_KV, rows, 1), F32),
                        pltpu.VMEM((N_KV, rows, HEAD_DIM), F32),
                        pltpu.VMEM((N_KV, rows, HEAD_DIM), F32), pltpu.VMEM((N_KV, rows, HEAD_DIM), F32),
                        pltpu.VMEM((npad, kv_new.shape[2]), F32)],
    )
    return pl.pallas_call(
        functools.partial(_nsa_paged_kernel, n_pg=n_pg, past_len=past_len),
        out_shape=jax.ShapeDtypeStruct(q.shape, F32),
        grid_spec=grid_spec,
        compiler_params=_params("parallel", "arbitrary"),
        name="nsa_paged",
    )(page_table, slopes, q, gates, cmp_kv, cmp_kv, kv_new, cache_win3, *([cache4] * n_pg))


def _alibi_slopes():
    n_heads = N_KV * GROUP
    m = np.exp2(-8.0 * np.arange(1, n_heads + 1) / n_heads)
    return jnp.asarray(m, dtype=F32).reshape(N_KV, GROUP)


def _gate_weights(w_qg, hd):
    nl, d, _ = w_qg.shape
    wg = w_qg[:, :, hd:].reshape(nl, d, 3, N_KV, GROUP).transpose(0, 1, 3, 2, 4).reshape(nl, d, N_KV, 3 * GROUP)
    wg = jnp.pad(wg, ((0, 0), (0, 0), (0, 0), (0, HEAD_DIM - 3 * GROUP)))
    return wg.reshape(nl, d, N_KV * HEAD_DIM)


def _forward(x, mod5, kvmod5, hist, paged, p, tiles):
    bsz, t, d = x.shape
    bb, tt = tiles["bb"], tiles["tt"]
    n_a = p["conv_w_in"].shape[0]
    depth = p["ffn_w_in"].shape[0]
    norm4 = p["norm_g"].reshape(depth, 3, 1, d)
    new_hist = []
    kv_all = None
    cmp_kv = None
    for l in range(depth):
        if l == n_a:
            kv_all = _kv_proj(x, kvmod5, p["kv_norm_g"], p["w_kv"], p["k_norm_g"], bb, tt)
            if paged is None:
                cmp_kv = _compress_prompt(kv_all, p["pos_t"], p["cmp_w1"], p["cmp_b1"], p["cmp_w2"], p["kgain0"])
            else:
                cmp_kv = _compress_paged(paged["cache4"], paged["page_table"], p["pos_t"], p["cmp_w1"],
                                         p["cmp_b1"], p["cmp_w2"], p["kgain0"], tiles["cmp_pages"])
        x = _ffn(x, mod5, norm4, p["ffn_w_in"], p["ffn_w_out"], l, 0, 0, 0, bb, tt, tiles["tf"], tiles["tn_ffn"])
        if l < n_a:
            u = _glu(x, mod5, norm4, p["conv_w_in"], p["conv_b_in"], l, bb, tt, tiles["tn"])
            hist_pad = jnp.pad(hist[l], ((0, 0), (HALO - (CONV_W - 1), 0), (0, 0)))
            x = _conv_tail(u, hist_pad, x, mod5, p["conv_dw"], p["conv_dw_b"], p["conv_ln_g"], p["conv_ln_b"],
                           p["conv_w_out"], p["conv_b_out"], l, bb, tt, tiles["tn"])
            new_hist.append(jnp.concatenate([hist[l], u], axis=1)[:, -(CONV_W - 1):])
        else:
            lj = l - n_a
            q, gates = _q_proj(x, mod5, norm4, p["w_q"], p["w_gate"], p["q_gain"], l, lj, bb, tt)
            if paged is None:
                a = _nsa_prompt(q, gates, cmp_kv, kv_all, p["slopes"], tiles["tq"], tiles["tk"])
            else:
                a = _nsa_paged(q, gates, cmp_kv, kv_all, paged["cache4"], paged["cache_win3"],
                               paged["page_table"], p["slopes"], tiles["nsa_pages"])
            x = _o_proj(a, x, mod5, p["w_o"], l, lj, bb, tt, tiles["tn"])
        x = _ffn(x, mod5, norm4, p["ffn_w_in"], p["ffn_w_out"], l, 1, 6, 2, bb, tt, tiles["tf"], tiles["tn_ffn"])
    return x, kv_all, jnp.stack(new_hist)


def kernel(x_prompt, x_sample, c_prompt, c_sample, cache_kv, cache_win, state_conv, page_table, ada_w, ada_b, norm_g, ffn_w_in, ffn_w_out, conv_w_in, conv_b_in, conv_dw, conv_dw_b, conv_ln_g, conv_ln_b, conv_w_out, conv_b_out, kv_norm_g, kv_ada_w, kv_ada_b, w_kv, cmp_pos, cmp_w1, cmp_b1, cmp_w2, k_norm_g, w_qg, q_norm_g, w_o):
    bp, tp, d = x_prompt.shape
    bs, ts, _ = x_sample.shape
    depth = ada_w.shape[0]
    hd = N_KV * GROUP * HEAD_DIM
    n_b = w_qg.shape[0]

    n_c = bp + bs
    c_all = jnp.pad(jnp.concatenate([c_prompt, c_sample], axis=0), ((0, -n_c % 16), (0, 0)))
    mod = _ada(c_all, ada_w, ada_b, 1024).reshape(depth, c_all.shape[0], 9, 1, d)
    kvmod = _ada(c_all, kv_ada_w[None], kv_ada_b[None], 1024).reshape(1, c_all.shape[0], 2, 1, d)

    p = dict(norm_g=norm_g, ffn_w_in=ffn_w_in.astype(BF), ffn_w_out=ffn_w_out.astype(BF),
             conv_w_in=conv_w_in.astype(BF), conv_b_in=conv_b_in, conv_dw=conv_dw, conv_dw_b=conv_dw_b,
             conv_ln_g=conv_ln_g, conv_ln_b=conv_ln_b, conv_w_out=conv_w_out.astype(BF), conv_b_out=conv_b_out,
             kv_norm_g=kv_norm_g, w_kv=w_kv.astype(BF), k_norm_g=k_norm_g,
             cmp_w1=cmp_w1.astype(BF), cmp_b1=cmp_b1, cmp_w2=cmp_w2,
             pos_t=jnp.transpose(cmp_pos, (1, 0, 2)),
             kgain0=k_norm_g[0].reshape(1, HEAD_DIM),
             w_q=w_qg[:, :, :hd].astype(BF), w_gate=_gate_weights(w_qg, hd).astype(BF),
             q_gain=jnp.tile(q_norm_g, (1, GROUP)).reshape(n_b, 1, GROUP * HEAD_DIM),
             w_o=w_o.astype(BF), slopes=_alibi_slopes())

    hist0 = jnp.zeros((conv_dw.shape[0], bp, CONV_W - 1, d), F32)
    tf = _largest_tile(ffn_w_out.shape[2], 1408)
    tiles_p = dict(bb=1, tt=512, tf=tf, tn=512, tn_ffn=1024, tq=128, tk=512)
    y_prompt, kv_p, conv_prompt = _forward(x_prompt, mod[:, :bp], kvmod[:, :bp], hist0, None, p, tiles_p)
    kv_prompt = kv_p[:, :, :4 * KV_SLOT].reshape(bp, tp, 4, N_KV, HEAD_DIM)
    wk = min(WINDOW, tp)
    win_prompt = kv_p[:, tp - wk:, 4 * KV_SLOT:].reshape(bp, wk, 2, N_KV, HEAD_DIM)

    n_pool, page = cache_kv.shape[0], cache_kv.shape[1]
    wlen = cache_win.shape[1]
    n_pages = page_table.shape[1]
    paged = dict(cache4=cache_kv.reshape(n_pool, page, 4 * N_KV, HEAD_DIM),
                 cache_win3=cache_win.reshape(bs, wlen, 2 * N_KV, HEAD_DIM),
                 page_table=page_table)
    tiles_s = dict(bb=bs, tt=ts, tf=tf, tn=512, tn_ffn=1024, cmp_pages=min(32, n_pages), nsa_pages=min(16, n_pages))
    y_sample, kv_s, conv_sample = _forward(x_sample, mod[:, bp:n_c], kvmod[:, bp:n_c], state_conv, paged, p, tiles_s)
    kv_sample = kv_s[:, :, :4 * KV_SLOT].reshape(bs, ts, 4, N_KV, HEAD_DIM)
    win_new = kv_s[:, :, 4 * KV_SLOT:].reshape(bs, ts, 2, N_KV, HEAD_DIM)
    win_sample = jnp.concatenate([cache_win, win_new], axis=1)[:, -wlen:]
    return (y_prompt, y_sample, kv_prompt, kv_sample, win_prompt, win_sample, conv_prompt, conv_sample)
```

```python
import functools
import math

import numpy as np
import jax
import jax.numpy as jnp
from jax import lax
from jax.experimental import pallas as pl
from jax.experimental.pallas import tpu as pltpu

HEAD_DIM = 128
N_KV = 4
GROUP = 4
BLK = 64
TOP_N = 16
WINDOW = 512
CONV_W = 31
HALO = 32
SUBLANES = 8
EPS = 1e-6
NEG = -1e30
BIG = 1e30
M_FLOOR = -1e29
FORCE = 1e4
HALF = 0.5
SCALE = HEAD_DIM ** -0.5
LOG2E = math.log2(math.e)
KV_SLOT = N_KV * HEAD_DIM

BF = jnp.bfloat16
F32 = jnp.float32

V7X_VMEM_BYTES = 64 * 1024 * 1024
VMEM_LIMIT = V7X_VMEM_BYTES - 8 * 1024 * 1024


def _largest_tile(n, cap):
    return max(c for c in range(128, min(n, cap) + 1, 128) if n % c == 0)


def _params(*sem):
    return pltpu.CompilerParams(dimension_semantics=sem, vmem_limit_bytes=VMEM_LIMIT)


def _dot(a, b):
    return jnp.dot(a, b, preferred_element_type=F32)


def _dot_nt(a, b):
    return lax.dot_general(a, b, (((1,), (1,)), ((), ())), preferred_element_type=F32)


def _dot_tn(a, b):
    return lax.dot_general(a, b, (((0,), (0,)), ((), ())), preferred_element_type=F32)


def _sigmoid(x):
    return jax.nn.sigmoid(x)


def _rms_groups(y, gain):
    outs = []
    for c in range(y.shape[-1] // HEAD_DIM):
        ch = y[:, c * HEAD_DIM:(c + 1) * HEAD_DIM]
        outs.append(ch * lax.rsqrt(jnp.mean(ch * ch, axis=-1, keepdims=True) + EPS))
    return jnp.concatenate(outs, axis=-1) * gain


def _modulate_into(h_ref, x_ref, g_ref, sh_ref, sc_ref):
    x = x_ref[...]
    bb, tt, d = x.shape
    y = x * lax.rsqrt(jnp.mean(x * x, axis=-1, keepdims=True) + EPS) * g_ref[...]
    h = y * (1.0 + sc_ref[...]) + sh_ref[...]
    h_ref[...] = h.reshape(bb * tt, d).astype(BF)


def _ada_kernel(c_ref, w_ref, b_ref, o_ref):
    c = c_ref[...]
    sc = (c * _sigmoid(c)).astype(BF)
    o_ref[...] = _dot(sc, w_ref[...].astype(BF)) + b_ref[...]


def _ada(c, w, b, tn):
    nl, k, n = w.shape
    m = c.shape[0]
    return pl.pallas_call(
        _ada_kernel,
        out_shape=jax.ShapeDtypeStruct((nl, m, n), F32),
        grid=(nl, n // tn),
        in_specs=[pl.BlockSpec((m, k), lambda l, j: (0, 0)),
                  pl.BlockSpec((None, k, tn), lambda l, j: (l, 0, j)),
                  pl.BlockSpec((None, 1, tn), lambda l, j: (l, 0, j))],
        out_specs=pl.BlockSpec((None, m, tn), lambda l, j: (l, 0, j)),
        compiler_params=_params("parallel", "parallel"),
        name="ada_proj",
    )(c, w, b.reshape(nl, 1, n))


def _x_spec(bb, tt, d):
    return pl.BlockSpec((bb, tt, d), lambda b, i, j: (b, i, 0))


def _mod_spec(bb, d, l, idx):
    return pl.BlockSpec((None, bb, None, 1, d), lambda b, i, j: (l, b, idx, 0, 0))


def _gain_spec(d, l, k):
    return pl.BlockSpec((None, None, 1, d), lambda b, i, j: (l, k, 0, 0))


def _ffn_up_kernel(x_ref, sh_ref, sc_ref, g_ref, wa_ref, wb_ref, o_ref, h_ref):
    @pl.when(pl.program_id(2) == 0)
    def _():
        _modulate_into(h_ref, x_ref, g_ref, sh_ref, sc_ref)

    h = h_ref[...]
    a = _dot(h, wa_ref[...])
    b = _dot(h, wb_ref[...])
    o_ref[...] = (a * _sigmoid(a) * b).astype(BF)


def _ffn_down_kernel(a_ref, x_ref, gt_ref, w_ref, o_ref):
    bb, tt, n = x_ref.shape
    out = _dot(a_ref[...], w_ref[...])
    o_ref[...] = x_ref[...] + HALF * gt_ref[...] * out.reshape(bb, tt, n)


def _ffn(x, mod5, norm4, w_in, w_out, l, sub, mod_base, norm_idx, bb, tt, tf, tn):
    bsz, t, d = x.shape
    f = w_out.shape[2]
    nf = f // tf
    nt = t // tt
    act = pl.pallas_call(
        _ffn_up_kernel,
        out_shape=jax.ShapeDtypeStruct((bsz * t, f), BF),
        grid=(bsz // bb, nt, nf),
        in_specs=[_x_spec(bb, tt, d),
                  _mod_spec(bb, d, l, mod_base), _mod_spec(bb, d, l, mod_base + 1),
                  _gain_spec(d, l, norm_idx),
                  pl.BlockSpec((None, None, d, tf), lambda b, i, j: (l, sub, 0, j)),
                  pl.BlockSpec((None, None, d, tf), lambda b, i, j: (l, sub, 0, nf + j))],
        out_specs=pl.BlockSpec((bb * tt, tf), lambda b, i, j: (b * nt + i, j)),
        scratch_shapes=[pltpu.VMEM((bb * tt, d), BF)],
        compiler_params=_params("parallel", "parallel", "arbitrary"),
        name="ffn_up",
    )(x, mod5, mod5, norm4, w_in, w_in)
    return pl.pallas_call(
        _ffn_down_kernel,
        out_shape=jax.ShapeDtypeStruct(x.shape, F32),
        grid=(bsz // bb, nt, d // tn),
        in_specs=[pl.BlockSpec((bb * tt, f), lambda b, i, j: (b * nt + i, 0)),
                  pl.BlockSpec((bb, tt, tn), lambda b, i, j: (b, i, j)),
                  pl.BlockSpec((None, bb, None, 1, tn), lambda b, i, j: (l, b, mod_base + 2, 0, j)),
                  pl.BlockSpec((None, None, f, tn), lambda b, i, j: (l, sub, 0, j))],
        out_specs=pl.BlockSpec((bb, tt, tn), lambda b, i, j: (b, i, j)),
        compiler_params=_params("parallel", "parallel", "arbitrary"),
        name="ffn_down",
    )(act, x, mod5, w_out)


def _glu_kernel(x_ref, sh_ref, sc_ref, g_ref, wa_ref, wg_ref, ba_ref, bg_ref, o_ref, h_ref):
    j = pl.program_id(2)
    bb, tt, _ = x_ref.shape

    @pl.when(j == 0)
    def _():
        _modulate_into(h_ref, x_ref, g_ref, sh_ref, sc_ref)

    h = h_ref[...]
    a = _dot(h, wa_ref[...]) + ba_ref[...]
    g = _dot(h, wg_ref[...]) + bg_ref[...]
    o_ref[...] = (a * _sigmoid(g)).reshape(bb, tt, a.shape[-1])


def _glu(x, mod5, norm4, w_in, b_in, l, bb, tt, tn):
    bsz, t, d = x.shape
    nn = d // tn
    b3 = b_in.reshape(b_in.shape[0], 1, 2 * d)
    return pl.pallas_call(
        _glu_kernel,
        out_shape=jax.ShapeDtypeStruct(x.shape, F32),
        grid=(bsz // bb, t // tt, nn),
        in_specs=[_x_spec(bb, tt, d),
                  _mod_spec(bb, d, l, 3), _mod_spec(bb, d, l, 4),
                  _gain_spec(d, l, 1),
                  pl.BlockSpec((None, d, tn), lambda b, i, j: (l, 0, j)),
                  pl.BlockSpec((None, d, tn), lambda b, i, j: (l, 0, nn + j)),
                  pl.BlockSpec((None, 1, tn), lambda b, i, j: (l, 0, j)),
                  pl.BlockSpec((None, 1, tn), lambda b, i, j: (l, 0, nn + j))],
        out_specs=pl.BlockSpec((bb, tt, tn), lambda b, i, j: (b, i, j)),
        scratch_shapes=[pltpu.VMEM((bb * tt, d), BF)],
        compiler_params=_params("parallel", "parallel", "arbitrary"),
        name="conv_glu",
    )(x, mod5, mod5, norm4, w_in, w_in, b3, b3)


def _conv_tail_kernel(u_ref, halo_ref, hist_ref, x_ref, gt_ref, dw_ref, dwb_ref, lng_ref, lnb_ref,
                      wo_ref, bo_ref, o_ref, ext_ref, z_ref, *, rows_per_step):
    i = pl.program_id(1)
    j = pl.program_id(2)
    bb, tt, d = u_ref.shape
    rps = rows_per_step

    @pl.when(j == 0)
    def _():
        ext_ref[:, 0:HALO, :] = jnp.where(i == 0, hist_ref[...], halo_ref[...])
        ext_ref[:, HALO:HALO + tt, :] = u_ref[...]
        off = HALO - (CONV_W - 1)

        def rows(c, carry):
            r0 = pl.multiple_of(c * rps, rps)
            parts = []
            for c0 in range(0, d, HEAD_DIM):
                lanes = slice(c0, c0 + HEAD_DIM)
                win = ext_ref[:, pl.ds(r0, rps + HALO), lanes]
                part = jnp.zeros((bb, rps, HEAD_DIM), F32) + dwb_ref[:, lanes]
                for s in range(SUBLANES):
                    taps = [w for w in range(CONV_W) if (off + w) % SUBLANES == s]
                    if s == 0:
                        shifted = win
                    else:
                        shifted = jnp.stack([pltpu.roll(win[bi], rps + HALO - s, axis=0) for bi in range(bb)])
                    for w in taps:
                        a0 = (off + w) // SUBLANES * SUBLANES
                        part = part + dw_ref[w:w + 1, lanes] * shifted[:, a0:a0 + rps, :]
                parts.append(part)
            acc = jnp.concatenate(parts, axis=-1)
            mu = jnp.mean(acc, axis=-1, keepdims=True)
            cen = acc - mu
            var = jnp.mean(cen * cen, axis=-1, keepdims=True)
            y = cen * lax.rsqrt(var + EPS) * lng_ref[...] + lnb_ref[...]
            z = y * _sigmoid(y)
            for bi in range(bb):
                z_ref[pl.ds(bi * tt + r0, rps), :] = z[bi]
            return carry

        lax.fori_loop(0, tt // rps, rows, 0)

    out = _dot(z_ref[...].astype(BF), wo_ref[...]) + bo_ref[...]
    o_ref[...] = x_ref[...] + gt_ref[...] * out.reshape(bb, tt, out.shape[-1])


def _conv_tail(u, hist_pad, x, mod5, dw, dw_b, ln_g, ln_b, w_out, b_out, l, bb, tt, tn):
    bsz, t, d = x.shape
    nl = dw.shape[0]
    if t > tt:
        per = tt // HALO
        halo_arr = u
        halo_spec = pl.BlockSpec((bb, HALO, d), lambda b, i, j: (b, jnp.maximum(i * per - 1, 0), 0))
    else:
        halo_arr = hist_pad
        halo_spec = pl.BlockSpec((bb, HALO, d), lambda b, i, j: (b, 0, 0))
    vec = lambda a: a.reshape(nl, 1, d)
    row_spec = pl.BlockSpec((None, 1, d), lambda b, i, j: (l, 0, 0))
    return pl.pallas_call(
        functools.partial(_conv_tail_kernel, rows_per_step=min(tt, 64)),
        out_shape=jax.ShapeDtypeStruct(x.shape, F32),
        grid=(bsz // bb, t // tt, d // tn),
        in_specs=[_x_spec(bb, tt, d),
                  halo_spec,
                  pl.BlockSpec((bb, HALO, d), lambda b, i, j: (b, 0, 0)),
                  pl.BlockSpec((bb, tt, tn), lambda b, i, j: (b, i, j)),
                  pl.BlockSpec((None, bb, None, 1, tn), lambda b, i, j: (l, b, 5, 0, j)),
                  pl.BlockSpec((None, CONV_W, d), lambda b, i, j: (l, 0, 0)),
                  row_spec, row_spec, row_spec,
                  pl.BlockSpec((None, d, tn), lambda b, i, j: (l, 0, j)),
                  pl.BlockSpec((None, 1, tn), lambda b, i, j: (l, 0, j))],
        out_specs=pl.BlockSpec((bb, tt, tn), lambda b, i, j: (b, i, j)),
        scratch_shapes=[pltpu.VMEM((bb, HALO + tt, d), F32), pltpu.VMEM((bb * tt, d), F32)],
        compiler_params=_params("parallel", "parallel", "arbitrary"),
        name="conv_tail",
    )(u, halo_arr, hist_pad, x, mod5, dw, vec(dw_b), vec(ln_g), vec(ln_b), w_out, vec(b_out))


def _kv_kernel(x_ref, sh_ref, sc_ref, g_ref, w_ref, kg_ref, nf_ref, o_ref, h_ref):
    bb, tt, _ = x_ref.shape

    @pl.when(pl.program_id(2) == 0)
    def _():
        _modulate_into(h_ref, x_ref, g_ref, sh_ref, sc_ref)

    y = _dot(h_ref[...], w_ref[...])
    y = jnp.where(nf_ref[...] > 0.5, _rms_groups(y, kg_ref[...]), y)
    o_ref[...] = y.reshape(bb, tt, y.shape[-1])


def _kv_proj(x, kvmod5, kv_norm_g, w_kv, k_norm_g, bb, tt, tn):
    bsz, t, d = x.shape
    n = w_kv.shape[1]
    one = jnp.ones((KV_SLOT,), F32)
    zero = jnp.zeros((KV_SLOT,), F32)
    gains = jnp.concatenate([one, one, jnp.tile(k_norm_g[1], N_KV), one, jnp.tile(k_norm_g[2], N_KV), one])
    normed = jnp.concatenate([zero, zero, one, zero, one, zero])
    return pl.pallas_call(
        _kv_kernel,
        out_shape=jax.ShapeDtypeStruct((bsz, t, n), F32),
        grid=(bsz // bb, t // tt, n // tn),
        in_specs=[_x_spec(bb, tt, d),
                  _mod_spec(bb, d, 0, 0), _mod_spec(bb, d, 0, 1),
                  pl.BlockSpec((1, d), lambda b, i, j: (0, 0)),
                  pl.BlockSpec((d, tn), lambda b, i, j: (0, j)),
                  pl.BlockSpec((1, tn), lambda b, i, j: (0, j)),
                  pl.BlockSpec((1, tn), lambda b, i, j: (0, j))],
        out_specs=pl.BlockSpec((bb, tt, tn), lambda b, i, j: (b, i, j)),
        scratch_shapes=[pltpu.VMEM((bb * tt, d), BF)],
        compiler_params=_params("parallel", "parallel", "arbitrary"),
        name="kv_proj",
    )(x, kvmod5, kvmod5, kv_norm_g.reshape(1, d), w_kv, gains.reshape(1, n), normed.reshape(1, n))


def _q_kernel(x_ref, sh_ref, sc_ref, g_ref, wq_ref, wg_ref, qg_ref, q_ref, gate_ref, h_ref):
    j = pl.program_id(2)
    bb, tt, _ = x_ref.shape

    @pl.when(j == 0)
    def _():
        _modulate_into(h_ref, x_ref, g_ref, sh_ref, sc_ref)
        gl = _dot(h_ref[...], wg_ref[...])
        gate_ref[...] = _sigmoid(gl).reshape(bb, tt, gl.shape[-1])

    y = _dot(h_ref[...], wq_ref[...])
    q_ref[...] = _rms_groups(y, qg_ref[...]).reshape(bb, tt, y.shape[-1])


def _q_proj(x, mod5, norm4, w_q, w_gate, q_gain, l, lj, bb, tt, tn):
    bsz, t, d = x.shape
    hd = w_q.shape[2]
    ng = w_gate.shape[2]
    return pl.pallas_call(
        _q_kernel,
        out_shape=(jax.ShapeDtypeStruct((bsz, t, hd), F32), jax.ShapeDtypeStruct((bsz, t, ng), F32)),
        grid=(bsz // bb, t // tt, hd // tn),
        in_specs=[_x_spec(bb, tt, d),
                  _mod_spec(bb, d, l, 3), _mod_spec(bb, d, l, 4),
                  _gain_spec(d, l, 1),
                  pl.BlockSpec((None, d, tn), lambda b, i, j: (lj, 0, j)),
                  pl.BlockSpec((None, d, ng), lambda b, i, j: (lj, 0, 0)),
                  pl.BlockSpec((None, 1, tn), lambda b, i, j: (lj, 0, j))],
        out_specs=(pl.BlockSpec((bb, tt, tn), lambda b, i, j: (b, i, j)),
                   pl.BlockSpec((bb, tt, ng), lambda b, i, j: (b, i, 0))),
        scratch_shapes=[pltpu.VMEM((bb * tt, d), BF)],
        compiler_params=_params("parallel", "parallel", "arbitrary"),
        name="q_proj",
    )(x, mod5, mod5, norm4, w_q, w_gate, q_gain)


def _oproj_kernel(a_ref, x_ref, gt_ref, w_ref, o_ref):
    bb, tt, k = a_ref.shape
    out = _dot(a_ref[...].reshape(bb * tt, k).astype(BF), w_ref[...])
    o_ref[...] = x_ref[...] + gt_ref[...] * out.reshape(bb, tt, out.shape[-1])


def _o_proj(a, x, mod5, w_o, l, lj, bb, tt, tn):
    bsz, t, d = x.shape
    k = a.shape[2]
    return pl.pallas_call(
        _oproj_kernel,
        out_shape=jax.ShapeDtypeStruct(x.shape, F32),
        grid=(bsz // bb, t // tt, d // tn),
        in_specs=[_x_spec(bb, tt, k),
                  pl.BlockSpec((bb, tt, tn), lambda b, i, j: (b, i, j)),
                  pl.BlockSpec((None, bb, None, 1, tn), lambda b, i, j: (l, b, 5, 0, j)),
                  pl.BlockSpec((None, k, tn), lambda b, i, j: (lj, 0, j))],
        out_specs=pl.BlockSpec((bb, tt, tn), lambda b, i, j: (b, i, j)),
        compiler_params=_params("parallel", "parallel", "arbitrary"),
        name="o_proj",
    )(a, x, mod5, w_o)


def _compress_pair(rows_a, rows_b, j, pos_ref, w1_ref):
    lhs = jnp.concatenate(
        [jnp.concatenate([rows_a[g] + pos_ref[j:j + 1, :], rows_b[g] + pos_ref[j + 1:j + 2, :]], axis=-1)
         for g in range(N_KV)], axis=0)
    return _dot(lhs.astype(BF), w1_ref[j * HEAD_DIM:(j + 2) * HEAD_DIM, :])


def _compress_finish(acc, nb, b1_ref, w2_ref, kg_ref, is_key):
    hid = jax.nn.gelu(acc + b1_ref[...])
    out = _dot(hid.astype(BF), w2_ref[...].astype(BF))
    normed = out * lax.rsqrt(jnp.mean(out * out, axis=-1, keepdims=True) + EPS) * kg_ref[...]
    out = jnp.where(is_key, normed, out)
    return out.reshape(N_KV, nb, HEAD_DIM)


def _compress_prompt_kernel(r0_ref, r1_ref, r2_ref, r3_ref, pos_ref, w1_ref, b1_ref, w2_ref, kg_ref, o_ref):
    heads = [r0_ref, r1_ref, r2_ref, r3_ref]
    nb = r0_ref.shape[0] // BLK
    acc = jnp.zeros((N_KV * nb, w1_ref.shape[-1]), F32)
    for j in range(0, BLK, 2):
        acc = acc + _compress_pair([h[pl.ds(j, nb, stride=BLK), :] for h in heads],
                                   [h[pl.ds(j + 1, nb, stride=BLK), :] for h in heads], j, pos_ref, w1_ref)
    o_ref[...] = _compress_finish(acc, nb, b1_ref, w2_ref, kg_ref, pl.program_id(0) == 0)


def _compress_prompt(kv_all, pos_t, w1, b1, w2, kgain):
    bsz, t, _ = kv_all.shape
    nc = t // BLK
    kf = BLK * HEAD_DIM

    def head_spec(g):
        return pl.BlockSpec((None, nc * BLK, HEAD_DIM), lambda p, b: (b, 0, p * N_KV + g))

    return pl.pallas_call(
        _compress_prompt_kernel,
        out_shape=jax.ShapeDtypeStruct((2, bsz, N_KV, nc, HEAD_DIM), F32),
        grid=(2, bsz),
        in_specs=[head_spec(g) for g in range(N_KV)] + [
            pl.BlockSpec((None, BLK, HEAD_DIM), lambda p, b: (p, 0, 0)),
            pl.BlockSpec((None, kf, 2 * HEAD_DIM), lambda p, b: (p, 0, 0)),
            pl.BlockSpec((None, 1, 2 * HEAD_DIM), lambda p, b: (p, 0, 0)),
            pl.BlockSpec((None, 2 * HEAD_DIM, HEAD_DIM), lambda p, b: (p, 0, 0)),
            pl.BlockSpec((1, HEAD_DIM), lambda p, b: (0, 0))],
        out_specs=pl.BlockSpec((None, None, N_KV, nc, HEAD_DIM), lambda p, b: (p, b, 0, 0, 0)),
        compiler_params=_params("arbitrary", "arbitrary"),
        name="compress_prompt",
    )(kv_all, kv_all, kv_all, kv_all, pos_t, w1, b1.reshape(2, 1, -1), w2, kgain)


def _compress_paged_kernel(pt_ref, cache_ref, w1_hbm, pos_ref, b1_ref, w2_ref, kg_ref, o_ref,
                           rows_ref, w1_ref, sem, w1_sem):
    b = pl.program_id(0)
    s = pl.program_id(1)
    ns = pl.num_programs(1)
    step = b * ns + s
    n_steps = pl.num_programs(0) * ns
    page = cache_ref.shape[1]
    n_rows = rows_ref.shape[0] // 2
    n_pg = n_rows // page
    nb = n_rows // BLK
    n_sub = 2 * N_KV
    slot = step % 2
    base = pl.multiple_of(slot * n_rows, n_rows)

    def page_copy(bi, si, k, sl):
        src = cache_ref.at[pt_ref[bi, si * n_pg + k], :, pl.ds(0, n_sub), :]
        dst = rows_ref.at[pl.ds(pl.multiple_of(sl * n_rows + k * page, page), page), :, :]
        return pltpu.make_async_copy(src, dst, sem.at[sl])

    def start_step(bi, si, sl):
        def body(k, c):
            page_copy(bi, si, k, sl).start()
            return c
        lax.fori_loop(0, n_pg, body, 0)

    @pl.when(step == 0)
    def _():
        w1_copy = pltpu.make_async_copy(w1_hbm, w1_ref, w1_sem)
        w1_copy.start()
        start_step(b, s, slot)
        w1_copy.wait()

    @pl.when(step + 1 < n_steps)
    def _():
        nxt = step + 1
        start_step(nxt // ns, nxt % ns, 1 - slot)

    def wait_body(k, c):
        page_copy(b, s, k, slot).wait()
        return c

    lax.fori_loop(0, n_pg, wait_body, 0)

    def block_rows(j):
        return jnp.swapaxes(rows_ref[pl.ds(base + j, nb, stride=BLK), :, :], 0, 1)

    accs = [jnp.zeros((N_KV * nb, w1_ref.shape[-1]), F32) for _ in range(2)]
    for j in range(0, BLK, 2):
        xa = block_rows(j)
        xb = block_rows(j + 1)
        for p in range(2):
            heads = range(p * N_KV, (p + 1) * N_KV)
            accs[p] = accs[p] + _compress_pair([xa[h] for h in heads], [xb[h] for h in heads], j,
                                               pos_ref.at[p], w1_ref.at[p])
    for p in range(2):
        o_ref[p] = _compress_finish(accs[p], nb, b1_ref.at[p], w2_ref.at[p], kg_ref, p == 0)


def _compress_paged(cache4, page_table, pos_t, w1, b1, w2, kgain, pages_per_step):
    bsz, n_pages = page_table.shape
    page = cache4.shape[1]
    nb = pages_per_step * page // BLK
    nc = n_pages * page // BLK
    full = lambda a: pl.BlockSpec(a.shape, lambda b, s, pt: (0,) * a.ndim)
    b1r = b1.reshape(2, 1, -1)
    grid_spec = pltpu.PrefetchScalarGridSpec(
        num_scalar_prefetch=1,
        grid=(bsz, n_pages // pages_per_step),
        in_specs=[pl.BlockSpec(memory_space=pl.ANY), pl.BlockSpec(memory_space=pl.ANY),
                  full(pos_t), full(b1r), full(w2), full(kgain)],
        out_specs=pl.BlockSpec((2, None, N_KV, nb, HEAD_DIM), lambda b, s, pt: (0, b, 0, s, 0)),
        scratch_shapes=[pltpu.VMEM((2 * pages_per_step * page, 2 * N_KV, HEAD_DIM), F32),
                        pltpu.VMEM(w1.shape, w1.dtype),
                        pltpu.SemaphoreType.DMA((2,)), pltpu.SemaphoreType.DMA],
    )
    return pl.pallas_call(
        _compress_paged_kernel,
        out_shape=jax.ShapeDtypeStruct((2, bsz, N_KV, nc, HEAD_DIM), F32),
        grid_spec=grid_spec,
        compiler_params=_params("arbitrary", "arbitrary"),
        name="compress_paged",
    )(page_table, cache4, w1, pos_t, b1r, w2, kgain)


def _stack_heads(q_val):
    return jnp.concatenate([q_val[:, r * HEAD_DIM:(r + 1) * HEAD_DIM] for r in range(GROUP)], axis=0)


def _masked_scores(q, k_bf, valid, dist, slopes_ref, g):
    t = dist.shape[0]
    raw = _dot_nt(q, k_bf)
    return jnp.concatenate(
        [jnp.where(valid, raw[r * t:(r + 1) * t] * SCALE - slopes_ref[g, r] * dist, NEG) for r in range(GROUP)],
        axis=0)


def _softmax_full(s):
    m = jnp.maximum(jnp.max(s, axis=-1, keepdims=True), M_FLOOR)
    e = jnp.exp(s - m)
    return e, jnp.sum(e, axis=-1, keepdims=True)


def _online_update(s, v_bf, m_old, l_old, acc_old):
    m_new = jnp.maximum(m_old, jnp.max(s, axis=-1, keepdims=True))
    e = jnp.exp(s - jnp.maximum(m_new, M_FLOOR))
    alpha = jnp.exp(m_old - m_new)
    l_new = alpha * l_old + jnp.sum(e, axis=-1, keepdims=True)
    acc_new = alpha * acc_old + _dot(e.astype(BF), v_bf)
    return m_new, l_new, acc_new


def _head_sum(p, t):
    out = p[0:t]
    for r in range(1, GROUP):
        out = out + p[r * t:(r + 1) * t]
    return out


def _top_blocks(score, blk, k):
    n = score.shape[-1]
    sel = jnp.zeros(score.shape, F32)
    for _ in range(k):
        mx = jnp.max(score, axis=-1, keepdims=True)
        idx = jnp.min(jnp.where(score == mx, blk, float(n)), axis=-1, keepdims=True)
        hit = blk == idx
        sel = jnp.where(hit & (mx >= 0.0), 1.0, sel)
        score = jnp.where(hit, -2.0, score)
    return sel


def _rank_select_t(score_t, k):
    n, t = score_t.shape
    groups = [score_t[r0:r0 + SUBLANES] for r0 in range(0, n, SUBLANES)]
    ranks = [jnp.zeros((SUBLANES, t), F32) for _ in groups]
    sub = lax.broadcasted_iota(jnp.int32, (SUBLANES, t), 0)
    for j in range(n):
        sj = score_t[j:j + 1, :]
        for gi, grp in enumerate(groups):
            lo = gi * SUBLANES
            ge = jnp.where(sj >= grp, 1.0, 0.0)
            gt = jnp.where(sj > grp, 1.0, 0.0)
            if lo > j:
                beats = ge
            elif lo + SUBLANES - 1 <= j:
                beats = gt
            else:
                beats = jnp.where(sub > j - lo, ge, gt)
            ranks[gi] = ranks[gi] + beats
    rank = jnp.concatenate(ranks, axis=0)
    return jnp.where((rank < float(k)) & (score_t >= 0.0), 1.0, 0.0)


def _expand_matrix(nsel, first_blk, n_keys):
    row = lax.broadcasted_iota(jnp.int32, (nsel, n_keys), 0)
    col = lax.broadcasted_iota(jnp.int32, (nsel, n_keys), 1)
    return jnp.where(row == first_blk + col // BLK, 1.0, 0.0).astype(BF)


def _combine(gate, o_cmp, o_sel, o_win, t):
    cols = []
    for r in range(GROUP):
        rs = slice(r * t, (r + 1) * t)
        cols.append(gate[:, r:r + 1] * o_cmp[rs] + gate[:, GROUP + r:GROUP + r + 1] * o_sel[rs]
                    + gate[:, 2 * GROUP + r:2 * GROUP + r + 1] * o_win[rs])
    return jnp.concatenate(cols, axis=-1)


def _nsa_prompt_kernel(slopes_ref, q_ref, gate_ref, kc_ref, vc_ref, ks_ref, vs_ref, kw_ref, vw_ref, relb_ref,
                       o_ref, m_ref, l_ref, acc_ref, any_ref, *, tk):
    g = pl.program_id(1)
    qi = pl.program_id(2)
    tq = q_ref.shape[0]
    nc = kc_ref.shape[0]
    rows = GROUP * tq
    t0 = qi * tq
    q = _stack_heads(q_ref[...]).astype(BF)
    c1 = SCALE * LOG2E

    blk_t = lax.broadcasted_iota(jnp.int32, (nc, tq), 0)
    tok_t = t0 + lax.broadcasted_iota(jnp.int32, (nc, tq), 1)
    dist_t = (tok_t - (blk_t * BLK + (BLK - 1))).astype(F32)
    valid_t = dist_t >= 0
    raw_t = _dot_nt(kc_ref[...].astype(BF), q)
    p_heads = []
    for r in range(GROUP):
        s = jnp.where(valid_t, raw_t[:, r * tq:(r + 1) * tq] * SCALE - slopes_ref[g, r] * dist_t, NEG)
        m = jnp.maximum(jnp.max(s, axis=0, keepdims=True), M_FLOOR)
        e = jnp.exp(s - m)
        p_heads.append(e / jnp.maximum(jnp.sum(e, axis=0, keepdims=True), 1e-30))
    o_cmp = _dot_tn(jnp.concatenate(p_heads, axis=1).astype(BF), vc_ref[...].astype(BF))
    imp_t = p_heads[0]
    for r in range(1, GROUP):
        imp_t = imp_t + p_heads[r]

    cur_t = tok_t // BLK
    forced = (blk_t == 0) | (blk_t == cur_t) | (blk_t == cur_t - 1)
    score_t = jnp.where(forced, FORCE, imp_t)
    score_t = jnp.where(blk_t <= cur_t, score_t, -1.0)
    sel_t = _rank_select_t(score_t, min(TOP_N, nc))
    unsel = ((1.0 - sel_t) * BIG).T.astype(BF)

    m_ref[...] = jnp.full(m_ref.shape, NEG, F32)
    l_ref[...] = jnp.zeros(l_ref.shape, F32)
    acc_ref[...] = jnp.zeros(acc_ref.shape, F32)
    rel = lax.broadcasted_iota(jnp.int32, (tq, tk), 0) - lax.broadcasted_iota(jnp.int32, (tq, tk), 1)
    slope_col = jnp.concatenate([jnp.full((tq, 1), slopes_ref[g, r] * LOG2E, F32) for r in range(GROUP)], axis=0)

    bpt = tk // BLK
    blk_any = jnp.max(sel_t, axis=1, keepdims=True)
    for kt in range(nc // bpt):
        any_ref[kt] = jnp.max(blk_any[kt * bpt:(kt + 1) * bpt])

    def key_tile(kt, causal):
        k0 = pl.multiple_of(kt * tk, tk)
        raw = _dot_nt(q, ks_ref[pl.ds(k0, tk), :].astype(BF))
        mask = _dot(unsel, _expand_matrix(nc, kt * bpt, tk))
        if causal:
            mask = mask + jnp.where(rel + (t0 - k0) >= 0, 0.0, BIG)
        s = jnp.concatenate([raw[r * tq:(r + 1) * tq] * c1 - relb_ref[r * tq:(r + 1) * tq, 0:tk] - mask
                             for r in range(GROUP)], axis=0)
        col = slope_col * (t0 - k0).astype(F32)
        m_old = m_ref[...]
        m_new = jnp.maximum(m_old, jnp.max(s, axis=-1, keepdims=True) - col)
        e = jnp.exp2(s - (jnp.maximum(m_new, M_FLOOR) + col))
        alpha = jnp.exp2(m_old - m_new)
        l_ref[...] = alpha * l_ref[...] + jnp.sum(e, axis=-1, keepdims=True)
        acc_ref[...] = alpha * acc_ref[...] + _dot(e.astype(BF), vs_ref[pl.ds(k0, tk), :].astype(BF))
        m_ref[...] = m_new

    k_diag = t0 // tk

    def early_tile(kt, carry):
        @pl.when(any_ref[kt] > 0.5)
        def _():
            key_tile(kt, causal=False)
        return carry

    lax.fori_loop(0, k_diag, early_tile, 0)
    key_tile(k_diag, causal=True)
    o_sel = acc_ref[...] / jnp.maximum(l_ref[...], 1e-30)

    lw = WINDOW + tq
    w0 = pl.multiple_of(jnp.maximum(t0 - WINDOW, 0), SUBLANES)
    dist = (t0 - w0) + (lax.broadcasted_iota(jnp.int32, (tq, lw), 0) - lax.broadcasted_iota(jnp.int32, (tq, lw), 1))
    mask = jnp.where((dist >= 0) & (dist < WINDOW), 0.0, BIG)
    raw = _dot_nt(q, kw_ref[pl.ds(w0, lw), :].astype(BF))
    s = jnp.concatenate([raw[r * tq:(r + 1) * tq] * c1 - relb_ref[r * tq:(r + 1) * tq, 0:lw] - mask
                         for r in range(GROUP)], axis=0)
    m = jnp.maximum(jnp.max(s, axis=-1, keepdims=True), M_FLOOR)
    e = jnp.exp2(s - m)
    o_win = _dot(e.astype(BF), vw_ref[pl.ds(w0, lw), :].astype(BF)) / jnp.maximum(
        jnp.sum(e, axis=-1, keepdims=True), 1e-30)

    o_ref[...] = _combine(gate_ref[...], o_cmp, o_sel, o_win, tq)


def _nsa_prompt(q, gates, cmp_kv, kv_all, slopes, tq, tk):
    bsz, t, hd = q.shape
    nc = cmp_kv.shape[3]
    lw = WINDOW + tq
    assert t % BLK == 0 and t % tk == 0 and t >= lw and tk <= lw and tk % tq == 0 and tk % BLK == 0
    rows = GROUP * tq
    rel = (jnp.arange(tq, dtype=F32)[:, None] - jnp.arange(lw, dtype=F32)[None, :])
    relb = ((slopes * LOG2E)[:, :, None, None] * rel[None, None]).reshape(N_KV, rows, lw)

    def col(slot):
        return pl.BlockSpec((None, t, HEAD_DIM), lambda b, g, i: (b, 0, slot * N_KV + g))

    def cmp_spec(p):
        return pl.BlockSpec((None, None, None, nc, HEAD_DIM), lambda b, g, i: (p, b, g, 0, 0))

    return pl.pallas_call(
        functools.partial(_nsa_prompt_kernel, tk=tk),
        out_shape=jax.ShapeDtypeStruct(q.shape, F32),
        grid=(bsz, N_KV, t // tq),
        in_specs=[pl.BlockSpec(memory_space=pltpu.SMEM),
                  pl.BlockSpec((None, tq, GROUP * HEAD_DIM), lambda b, g, i: (b, i, g)),
                  pl.BlockSpec((None, tq, HEAD_DIM), lambda b, g, i: (b, i, g)),
                  cmp_spec(0), cmp_spec(1), col(2), col(3), col(4), col(5),
                  pl.BlockSpec((None, rows, lw), lambda b, g, i: (g, 0, 0))],
        out_specs=pl.BlockSpec((None, tq, GROUP * HEAD_DIM), lambda b, g, i: (b, i, g)),
        scratch_shapes=[pltpu.VMEM((rows, 1), F32), pltpu.VMEM((rows, 1), F32),
                        pltpu.VMEM((rows, HEAD_DIM), F32), pltpu.SMEM((t // tk,), F32)],
        compiler_params=_params("parallel", "parallel", "arbitrary"),
        name="nsa_prompt",
    )(slopes, q, gates, cmp_kv, cmp_kv, kv_all, kv_all, kv_all, kv_all, relb)


def _nsa_paged_kernel(pt_ref, slopes_ref, q_ref, gate_ref, kc_ref, vc_ref, new_ref, cwin_ref, *rest,
                      n_pg, past_len):
    page_refs = rest[:n_pg]
    o_ref, sel_ref, m_ref, l_ref, acc_ref, ocmp_ref, owin_ref, pad_ref = rest[n_pg:]
    s_idx = pl.program_id(1)
    tq = q_ref.shape[0]
    nc = kc_ref.shape[1]
    page = page_refs[0].shape[0]
    rows = GROUP * tq
    npad = pad_ref.shape[0]
    wlen = cwin_ref.shape[0]

    def q_rows(g):
        return _stack_heads(q_ref[:, g * KV_SLOT:(g + 1) * KV_SLOT]).astype(BF)

    def tok_pos(n):
        return past_len + lax.broadcasted_iota(jnp.int32, (tq, n), 0)

    def new_cols(slot, g):
        return pad_ref[:, slot * KV_SLOT + g * HEAD_DIM:slot * KV_SLOT + (g + 1) * HEAD_DIM]

    def by_head(ref):
        return jnp.swapaxes(ref[...], 0, 1)

    @pl.when(s_idx == 0)
    def _():
        pad_ref[...] = jnp.zeros(pad_ref.shape, F32)
        pad_ref[0:tq, :] = new_ref[...]
        new_i = lax.broadcasted_iota(jnp.int32, (tq, npad), 1)
        blk_i = lax.broadcasted_iota(jnp.int32, (tq, nc), 1)
        cwin = by_head(cwin_ref)
        imps = []
        for g in range(N_KV):
            q = q_rows(g)
            dist = (tok_pos(nc) - (blk_i * BLK + (BLK - 1))).astype(F32)
            e, l = _softmax_full(_masked_scores(q, kc_ref[g].astype(BF), dist >= 0, dist, slopes_ref, g))
            p_cmp = e / jnp.maximum(l, 1e-30)
            ocmp_ref[g] = _dot(p_cmp.astype(BF), vc_ref[g].astype(BF))
            imps.append(_head_sum(p_cmp, tq))
            dist = (tok_pos(npad) - (past_len + new_i)).astype(F32)
            valid = (dist >= 0) & (new_i < tq)
            s = _masked_scores(q, new_cols(2, g).astype(BF), valid, dist, slopes_ref, g)
            m, l, acc = _online_update(s, new_cols(3, g).astype(BF), jnp.full((rows, 1), NEG, F32),
                                       jnp.zeros((rows, 1), F32), jnp.zeros((rows, HEAD_DIM), F32))
            m_ref[g] = m
            l_ref[g] = l
            acc_ref[g] = acc
            kw = jnp.concatenate([cwin[g], new_cols(4, g)], axis=0)
            vw = jnp.concatenate([cwin[N_KV + g], new_cols(5, g)], axis=0)
            win_i = lax.broadcasted_iota(jnp.int32, (tq, wlen + npad), 1)
            dist = (tok_pos(wlen + npad) - (past_len - wlen + win_i)).astype(F32)
            valid = (dist >= 0) & (dist < WINDOW) & (win_i < wlen + tq)
            e, l = _softmax_full(_masked_scores(q, kw.astype(BF), valid, dist, slopes_ref, g))
            owin_ref[g] = _dot(e.astype(BF), vw.astype(BF)) / jnp.maximum(l, 1e-30)
        blk_all = lax.broadcasted_iota(jnp.int32, (N_KV * tq, nc), 1)
        score = jnp.where((blk_all == 0) | (blk_all == nc - 1), FORCE, jnp.concatenate(imps, axis=0))
        sel = _top_blocks(score, blk_all.astype(F32), min(TOP_N, nc + 1) - 1)
        for g in range(N_KV):
            sel_ref[g] = sel[g * tq:(g + 1) * tq]

    nk = n_pg * page
    k0 = s_idx * nk
    kpos = k0 + lax.broadcasted_iota(jnp.int32, (tq, nk), 1)
    dist = (tok_pos(nk) - kpos).astype(F32)
    expand = _expand_matrix(nc, s_idx * (nk // BLK), nk)
    pages = [by_head(pr) for pr in page_refs]
    for g in range(N_KV):
        q = q_rows(g)
        ks = jnp.concatenate([pg[g] for pg in pages], axis=0)
        vs = jnp.concatenate([pg[N_KV + g] for pg in pages], axis=0)
        valid = (_dot(sel_ref[g].astype(BF), expand) > 0.5) & (dist >= 0)
        s = _masked_scores(q, ks.astype(BF), valid, dist, slopes_ref, g)
        m, l, acc = _online_update(s, vs.astype(BF), m_ref[g], l_ref[g], acc_ref[g])
        m_ref[g] = m
        l_ref[g] = l
        acc_ref[g] = acc

    @pl.when(s_idx == pl.num_programs(1) - 1)
    def _():
        outs = []
        for g in range(N_KV):
            o_sel = acc_ref[g] / jnp.maximum(l_ref[g], 1e-30)
            outs.append(_combine(gate_ref[:, g * HEAD_DIM:(g + 1) * HEAD_DIM], ocmp_ref[g], o_sel, owin_ref[g], tq))
        o_ref[...] = jnp.concatenate(outs, axis=-1)


def _nsa_paged(q, gates, cmp_kv, kv_new, cache4, cache_win3, page_table, slopes, n_pg):
    bsz, tq, hd = q.shape
    n_pages = page_table.shape[1]
    page = cache4.shape[1]
    past_len = n_pages * page
    nc = cmp_kv.shape[3]
    wlen = cache_win3.shape[1]
    assert past_len % BLK == 0 and tq < BLK and nc == past_len // BLK and n_pages % n_pg == 0
    assert wlen == WINDOW and tq % SUBLANES == 0
    rows = GROUP * tq
    npad = 128

    def cmp_spec(p):
        return pl.BlockSpec((None, None, N_KV, nc, HEAD_DIM), lambda b, s, pt: (p, b, 0, 0, 0))

    def page_spec(k):
        return pl.BlockSpec((None, page, 2 * N_KV, HEAD_DIM), lambda b, s, pt: (pt[b, s * n_pg + k], 0, 1, 0))

    grid_spec = pltpu.PrefetchScalarGridSpec(
        num_scalar_prefetch=1,
        grid=(bsz, n_pages // n_pg),
        in_specs=[pl.BlockSpec(memory_space=pltpu.SMEM),
                  pl.BlockSpec((None, tq, hd), lambda b, s, pt: (b, 0, 0)),
                  pl.BlockSpec((None, tq, gates.shape[2]), lambda b, s, pt: (b, 0, 0)),
                  cmp_spec(0), cmp_spec(1),
                  pl.BlockSpec((None, tq, kv_new.shape[2]), lambda b, s, pt: (b, 0, 0)),
                  pl.BlockSpec((None, wlen, 2 * N_KV, HEAD_DIM), lambda b, s, pt: (b, 0, 0, 0))]
                 + [page_spec(k) for k in range(n_pg)],
        out_specs=pl.BlockSpec((None, tq, hd), lambda b, s, pt: (b, 0, 0)),
        scratch_shapes=[pltpu.VMEM((N_KV, tq, nc), F32),
                        pltpu.VMEM((N_KV, rows, 1), F32), pltpu.VMEM((N_KV, rows, 1), F32),
                        pltpu.VMEM((N_KV, rows, HEAD_DIM), F32),
                        pltpu.VMEM((N_KV, rows, HEAD_DIM), F32), pltpu.VMEM((N_KV, rows, HEAD_DIM), F32),
                        pltpu.VMEM((npad, kv_new.shape[2]), F32)],
    )
    return pl.pallas_call(
        functools.partial(_nsa_paged_kernel, n_pg=n_pg, past_len=past_len),
        out_shape=jax.ShapeDtypeStruct(q.shape, F32),
        grid_spec=grid_spec,
        compiler_params=_params("parallel", "arbitrary"),
        name="nsa_paged",
    )(page_table, slopes, q, gates, cmp_kv, cmp_kv, kv_new, cache_win3, *([cache4] * n_pg))


def _alibi_slopes():
    n_heads = N_KV * GROUP
    m = np.exp2(-8.0 * np.arange(1, n_heads + 1) / n_heads)
    return jnp.asarray(m, dtype=F32).reshape(N_KV, GROUP)


def _gate_weights(w_qg, hd):
    nl, d, _ = w_qg.shape
    wg = w_qg[:, :, hd:].reshape(nl, d, 3, N_KV, GROUP).transpose(0, 1, 3, 2, 4).reshape(nl, d, N_KV, 3 * GROUP)
    wg = jnp.pad(wg, ((0, 0), (0, 0), (0, 0), (0, HEAD_DIM - 3 * GROUP)))
    return wg.reshape(nl, d, N_KV * HEAD_DIM)


def _forward(x, mod5, kvmod5, hist, paged, p, tiles):
    bsz, t, d = x.shape
    bb, tt, tm, tn = tiles["bb"], tiles["tt"], tiles["tm"], tiles["tn"]
    n_a = p["conv_w_in"].shape[0]
    depth = p["ffn_w_in"].shape[0]
    norm4 = p["norm_g"].reshape(depth, 3, 1, d)
    new_hist = []
    kv_all = None
    cmp_kv = None

    def ffn(x, l, sub):
        return _ffn(x, mod5, norm4, p["ffn_w_in"], p["ffn_w_out"], l, sub, 6 * sub, 2 * sub, bb, tm,
                    tiles["tf"], tiles["tn_ffn"])

    for l in range(depth):
        if l == n_a:
            kv_all = _kv_proj(x, kvmod5, p["kv_norm_g"], p["w_kv"], p["k_norm_g"], bb, tm, tn)
            if paged is None:
                cmp_kv = _compress_prompt(kv_all, p["pos_t"], p["cmp_w1"], p["cmp_b1"], p["cmp_w2"], p["kgain0"])
            else:
                cmp_kv = _compress_paged(paged["cache4"], paged["page_table"], p["pos_t"], p["cmp_w1"],
                                         p["cmp_b1"], p["cmp_w2"], p["kgain0"], tiles["cmp_pages"])
        x = ffn(x, l, 0)
        if l < n_a:
            u = _glu(x, mod5, norm4, p["conv_w_in"], p["conv_b_in"], l, bb, tm, tn)
            hist_pad = jnp.pad(hist[l], ((0, 0), (HALO - (CONV_W - 1), 0), (0, 0)))
            x = _conv_tail(u, hist_pad, x, mod5, p["conv_dw"], p["conv_dw_b"], p["conv_ln_g"], p["conv_ln_b"],
                           p["conv_w_out"], p["conv_b_out"], l, bb, tt, tn)
            new_hist.append(jnp.concatenate([hist[l], u], axis=1)[:, -(CONV_W - 1):])
        else:
            lj = l - n_a
            q, gates = _q_proj(x, mod5, norm4, p["w_q"], p["w_gate"], p["q_gain"], l, lj, bb, tm, tn)
            if paged is None:
                a = _nsa_prompt(q, gates, cmp_kv, kv_all, p["slopes"], tiles["tq"], tiles["tk"])
            else:
                a = _nsa_paged(q, gates, cmp_kv, kv_all, paged["cache4"], paged["cache_win3"],
                               paged["page_table"], p["slopes"], tiles["nsa_pages"])
            x = _o_proj(a, x, mod5, p["w_o"], l, lj, bb, tm, tn)
        x = ffn(x, l, 1)
    return x, kv_all, jnp.stack(new_hist)


def kernel(x_prompt, x_sample, c_prompt, c_sample, cache_kv, cache_win, state_conv, page_table, ada_w, ada_b, norm_g, ffn_w_in, ffn_w_out, conv_w_in, conv_b_in, conv_dw, conv_dw_b, conv_ln_g, conv_ln_b, conv_w_out, conv_b_out, kv_norm_g, kv_ada_w, kv_ada_b, w_kv, cmp_pos, cmp_w1, cmp_b1, cmp_w2, k_norm_g, w_qg, q_norm_g, w_o):
    bp, tp, d = x_prompt.shape
    bs, ts, _ = x_sample.shape
    depth = ada_w.shape[0]
    hd = N_KV * GROUP * HEAD_DIM
    n_b = w_qg.shape[0]

    n_c = bp + bs
    c_all = jnp.pad(jnp.concatenate([c_prompt, c_sample], axis=0), ((0, -n_c % 16), (0, 0)))
    mod = _ada(c_all, ada_w, ada_b, 1024).reshape(depth, c_all.shape[0], 9, 1, d)
    kvmod = _ada(c_all, kv_ada_w[None], kv_ada_b[None], 1024).reshape(1, c_all.shape[0], 2, 1, d)

    p = dict(norm_g=norm_g, ffn_w_in=ffn_w_in.astype(BF), ffn_w_out=ffn_w_out.astype(BF),
             conv_w_in=conv_w_in.astype(BF), conv_b_in=conv_b_in, conv_dw=conv_dw, conv_dw_b=conv_dw_b,
             conv_ln_g=conv_ln_g, conv_ln_b=conv_ln_b, conv_w_out=conv_w_out.astype(BF), conv_b_out=conv_b_out,
             kv_norm_g=kv_norm_g, w_kv=w_kv.astype(BF), k_norm_g=k_norm_g,
             cmp_w1=cmp_w1.astype(BF), cmp_b1=cmp_b1, cmp_w2=cmp_w2,
             pos_t=jnp.transpose(cmp_pos, (1, 0, 2)),
             kgain0=k_norm_g[0].reshape(1, HEAD_DIM),
             w_q=w_qg[:, :, :hd].astype(BF), w_gate=_gate_weights(w_qg, hd).astype(BF),
             q_gain=jnp.tile(q_norm_g, (1, N_KV * GROUP)).reshape(n_b, 1, hd),
             w_o=w_o.astype(BF), slopes=_alibi_slopes())

    hist0 = jnp.zeros((conv_dw.shape[0], bp, CONV_W - 1, d), F32)
    d_ff = ffn_w_out.shape[2]
    tiles_p = dict(bb=1, tt=512, tm=min(1024, tp), tf=_largest_tile(d_ff, 512), tn=512, tn_ffn=512, tq=128, tk=512)
    y_prompt, kv_p, conv_prompt = _forward(x_prompt, mod[:, :bp], kvmod[:, :bp], hist0, None, p, tiles_p)
    kv_prompt = kv_p[:, :, :4 * KV_SLOT].reshape(bp, tp, 4, N_KV, HEAD_DIM)
    wk = min(WINDOW, tp)
    win_prompt = kv_p[:, tp - wk:, 4 * KV_SLOT:].reshape(bp, wk, 2, N_KV, HEAD_DIM)

    n_pool, page = cache_kv.shape[0], cache_kv.shape[1]
    wlen = cache_win.shape[1]
    n_pages = page_table.shape[1]
    paged = dict(cache4=cache_kv.reshape(n_pool, page, 4 * N_KV, HEAD_DIM),
                 cache_win3=cache_win.reshape(bs, wlen, 2 * N_KV, HEAD_DIM),
                 page_table=page_table)
    tiles_s = dict(bb=bs, tt=ts, tm=ts, tf=_largest_tile(d_ff, 1408), tn=512, tn_ffn=1024, cmp_pages=min(32, n_pages), nsa_pages=min(16, n_pages))
    y_sample, kv_s, conv_sample = _forward(x_sample, mod[:, bp:n_c], kvmod[:, bp:n_c], state_conv, paged, p, tiles_s)
    kv_sample = kv_s[:, :, :4 * KV_SLOT].reshape(bs, ts, 4, N_KV, HEAD_DIM)
    win_new = kv_s[:, :, 4 * KV_SLOT:].reshape(bs, ts, 2, N_KV, HEAD_DIM)
    win_sample = jnp.concatenate([cache_win, win_new], axis=1)[:, -wlen:]
    return (y_prompt, y_sample, kv_prompt, kv_sample, win_prompt, win_sample, conv_prompt, conv_sample)
```

```python
import functools
import math

import numpy as np
import jax
import jax.numpy as jnp
from jax import lax
from jax.experimental import pallas as pl
from jax.experimental.pallas import tpu as pltpu

HEAD_DIM = 128
N_KV = 4
GROUP = 4
BLK = 64
TOP_N = 16
WINDOW = 512
CONV_W = 31
HALO = 32
SUBLANES = 8
EPS = 1e-6
NEG = -1e30
BIG = 1e30
M_FLOOR = -1e29
FORCE = 1e4
HALF = 0.5
SCALE = HEAD_DIM ** -0.5
LOG2E = math.log2(math.e)
KV_SLOT = N_KV * HEAD_DIM

BF = jnp.bfloat16
F32 = jnp.float32

V7X_VMEM_BYTES = 64 * 1024 * 1024
VMEM_LIMIT = V7X_VMEM_BYTES - 8 * 1024 * 1024


def _largest_tile(n, cap):
    return max(c for c in range(128, min(n, cap) + 1, 128) if n % c == 0)


def _tile_plans(t_prompt, t_sample, b_sample, d_ff, n_pages):
    prompt = dict(bb=1, tt=min(512, t_prompt), tm=min(1024, t_prompt), tf=_largest_tile(d_ff, 512), tn=512,
                  tn_ffn=512, tq=128, tk=512)
    sample = dict(bb=b_sample, tt=t_sample, tm=t_sample, tf=_largest_tile(d_ff, 1408), tn=512, tn_ffn=1024,
                  cmp_pages=min(32, n_pages), nsa_pages=min(16, n_pages))
    return prompt, sample


def _params(*sem):
    return pltpu.CompilerParams(dimension_semantics=sem, vmem_limit_bytes=VMEM_LIMIT)


def _dot(a, b):
    return jnp.dot(a, b, preferred_element_type=F32)


def _dot_nt(a, b):
    return lax.dot_general(a, b, (((1,), (1,)), ((), ())), preferred_element_type=F32)


def _dot_tn(a, b):
    return lax.dot_general(a, b, (((0,), (0,)), ((), ())), preferred_element_type=F32)


def _sigmoid(x):
    return jax.nn.sigmoid(x)


def _rms_groups(y, gain):
    outs = []
    for c in range(y.shape[-1] // HEAD_DIM):
        ch = y[:, c * HEAD_DIM:(c + 1) * HEAD_DIM]
        outs.append(ch * lax.rsqrt(jnp.mean(ch * ch, axis=-1, keepdims=True) + EPS))
    return jnp.concatenate(outs, axis=-1) * gain


def _modulate_into(h_ref, x_ref, g_ref, sh_ref, sc_ref):
    x = x_ref[...]
    bb, tt, d = x.shape
    y = x * lax.rsqrt(jnp.mean(x * x, axis=-1, keepdims=True) + EPS) * g_ref[...]
    h = y * (1.0 + sc_ref[...]) + sh_ref[...]
    h_ref[...] = h.reshape(bb * tt, d).astype(BF)


def _ada_kernel(c_ref, w_ref, b_ref, o_ref):
    c = c_ref[...]
    sc = (c * _sigmoid(c)).astype(BF)
    o_ref[...] = _dot(sc, w_ref[...].astype(BF)) + b_ref[...]


def _ada(c, w, b, tn):
    nl, k, n = w.shape
    m = c.shape[0]
    return pl.pallas_call(
        _ada_kernel,
        out_shape=jax.ShapeDtypeStruct((nl, m, n), F32),
        grid=(nl, n // tn),
        in_specs=[pl.BlockSpec((m, k), lambda l, j: (0, 0)),
                  pl.BlockSpec((None, k, tn), lambda l, j: (l, 0, j)),
                  pl.BlockSpec((None, 1, tn), lambda l, j: (l, 0, j))],
        out_specs=pl.BlockSpec((None, m, tn), lambda l, j: (l, 0, j)),
        compiler_params=_params("parallel", "parallel"),
        name="ada_proj",
    )(c, w, b.reshape(nl, 1, n))


def _x_spec(bb, tt, d):
    return pl.BlockSpec((bb, tt, d), lambda b, i, j: (b, i, 0))


def _mod_spec(bb, d, l, idx):
    return pl.BlockSpec((None, bb, None, 1, d), lambda b, i, j: (l, b, idx, 0, 0))


def _gain_spec(d, l, k):
    return pl.BlockSpec((None, None, 1, d), lambda b, i, j: (l, k, 0, 0))


def _ffn_up_kernel(x_ref, sh_ref, sc_ref, g_ref, wa_ref, wb_ref, o_ref, h_ref):
    @pl.when(pl.program_id(2) == 0)
    def _():
        _modulate_into(h_ref, x_ref, g_ref, sh_ref, sc_ref)

    h = h_ref[...]
    a = _dot(h, wa_ref[...])
    b = _dot(h, wb_ref[...])
    o_ref[...] = (a * _sigmoid(a) * b).astype(BF)


def _ffn_down_kernel(a_ref, x_ref, gt_ref, w_ref, o_ref):
    bb, tt, n = x_ref.shape
    out = _dot(a_ref[...], w_ref[...])
    o_ref[...] = x_ref[...] + HALF * gt_ref[...] * out.reshape(bb, tt, n)


def _ffn(x, mod5, norm4, w_in, w_out, l, sub, mod_base, norm_idx, bb, tt, tf, tn):
    bsz, t, d = x.shape
    f = w_out.shape[2]
    nf = f // tf
    nt = t // tt
    act = pl.pallas_call(
        _ffn_up_kernel,
        out_shape=jax.ShapeDtypeStruct((bsz * t, f), BF),
        grid=(bsz // bb, nt, nf),
        in_specs=[_x_spec(bb, tt, d),
                  _mod_spec(bb, d, l, mod_base), _mod_spec(bb, d, l, mod_base + 1),
                  _gain_spec(d, l, norm_idx),
                  pl.BlockSpec((None, None, d, tf), lambda b, i, j: (l, sub, 0, j)),
                  pl.BlockSpec((None, None, d, tf), lambda b, i, j: (l, sub, 0, nf + j))],
        out_specs=pl.BlockSpec((bb * tt, tf), lambda b, i, j: (b * nt + i, j)),
        scratch_shapes=[pltpu.VMEM((bb * tt, d), BF)],
        compiler_params=_params("parallel", "parallel", "arbitrary"),
        name="ffn_up",
    )(x, mod5, mod5, norm4, w_in, w_in)
    return pl.pallas_call(
        _ffn_down_kernel,
        out_shape=jax.ShapeDtypeStruct(x.shape, F32),
        grid=(bsz // bb, nt, d // tn),
        in_specs=[pl.BlockSpec((bb * tt, f), lambda b, i, j: (b * nt + i, 0)),
                  pl.BlockSpec((bb, tt, tn), lambda b, i, j: (b, i, j)),
                  pl.BlockSpec((None, bb, None, 1, tn), lambda b, i, j: (l, b, mod_base + 2, 0, j)),
                  pl.BlockSpec((None, None, f, tn), lambda b, i, j: (l, sub, 0, j))],
        out_specs=pl.BlockSpec((bb, tt, tn), lambda b, i, j: (b, i, j)),
        compiler_params=_params("parallel", "parallel", "arbitrary"),
        name="ffn_down",
    )(act, x, mod5, w_out)


def _glu_kernel(x_ref, sh_ref, sc_ref, g_ref, wa_ref, wg_ref, ba_ref, bg_ref, o_ref, h_ref):
    j = pl.program_id(2)
    bb, tt, _ = x_ref.shape

    @pl.when(j == 0)
    def _():
        _modulate_into(h_ref, x_ref, g_ref, sh_ref, sc_ref)

    h = h_ref[...]
    a = _dot(h, wa_ref[...]) + ba_ref[...]
    g = _dot(h, wg_ref[...]) + bg_ref[...]
    o_ref[...] = (a * _sigmoid(g)).reshape(bb, tt, a.shape[-1])


def _glu(x, mod5, norm4, w_in, b_in, l, bb, tt, tn):
    bsz, t, d = x.shape
    nn = d // tn
    b3 = b_in.reshape(b_in.shape[0], 1, 2 * d)
    return pl.pallas_call(
        _glu_kernel,
        out_shape=jax.ShapeDtypeStruct(x.shape, F32),
        grid=(bsz // bb, t // tt, nn),
        in_specs=[_x_spec(bb, tt, d),
                  _mod_spec(bb, d, l, 3), _mod_spec(bb, d, l, 4),
                  _gain_spec(d, l, 1),
                  pl.BlockSpec((None, d, tn), lambda b, i, j: (l, 0, j)),
                  pl.BlockSpec((None, d, tn), lambda b, i, j: (l, 0, nn + j)),
                  pl.BlockSpec((None, 1, tn), lambda b, i, j: (l, 0, j)),
                  pl.BlockSpec((None, 1, tn), lambda b, i, j: (l, 0, nn + j))],
        out_specs=pl.BlockSpec((bb, tt, tn), lambda b, i, j: (b, i, j)),
        scratch_shapes=[pltpu.VMEM((bb * tt, d), BF)],
        compiler_params=_params("parallel", "parallel", "arbitrary"),
        name="conv_glu",
    )(x, mod5, mod5, norm4, w_in, w_in, b3, b3)


def _conv_tail_kernel(u_ref, halo_ref, hist_ref, x_ref, gt_ref, dw_ref, dwb_ref, lng_ref, lnb_ref,
                      wo_ref, bo_ref, o_ref, ext_ref, z_ref, *, rows_per_step):
    i = pl.program_id(1)
    j = pl.program_id(2)
    bb, tt, d = u_ref.shape
    rps = rows_per_step

    @pl.when(j == 0)
    def _():
        ext_ref[:, 0:HALO, :] = jnp.where(i == 0, hist_ref[...], halo_ref[...])
        ext_ref[:, HALO:HALO + tt, :] = u_ref[...]
        off = HALO - (CONV_W - 1)

        def rows(c, carry):
            r0 = pl.multiple_of(c * rps, rps)
            parts = []
            for c0 in range(0, d, HEAD_DIM):
                lanes = slice(c0, c0 + HEAD_DIM)
                win = ext_ref[:, pl.ds(r0, rps + HALO), lanes]
                part = jnp.zeros((bb, rps, HEAD_DIM), F32) + dwb_ref[:, lanes]
                for s in range(SUBLANES):
                    taps = [w for w in range(CONV_W) if (off + w) % SUBLANES == s]
                    if s == 0:
                        shifted = win
                    else:
                        shifted = jnp.stack([pltpu.roll(win[bi], rps + HALO - s, axis=0) for bi in range(bb)])
                    for w in taps:
                        a0 = (off + w) // SUBLANES * SUBLANES
                        part = part + dw_ref[w:w + 1, lanes] * shifted[:, a0:a0 + rps, :]
                parts.append(part)
            acc = jnp.concatenate(parts, axis=-1)
            mu = jnp.mean(acc, axis=-1, keepdims=True)
            cen = acc - mu
            var = jnp.mean(cen * cen, axis=-1, keepdims=True)
            y = cen * lax.rsqrt(var + EPS) * lng_ref[...] + lnb_ref[...]
            z = y * _sigmoid(y)
            for bi in range(bb):
                z_ref[pl.ds(bi * tt + r0, rps), :] = z[bi].astype(z_ref.dtype)
            return carry

        lax.fori_loop(0, tt // rps, rows, 0)

    out = _dot(z_ref[...].astype(BF), wo_ref[...]) + bo_ref[...]
    o_ref[...] = x_ref[...] + gt_ref[...] * out.reshape(bb, tt, out.shape[-1])


def _conv_tail(u, hist_pad, x, mod5, dw, dw_b, ln_g, ln_b, w_out, b_out, l, bb, tt, tn):
    bsz, t, d = x.shape
    nl = dw.shape[0]
    if t > tt:
        per = tt // HALO
        halo_arr = u
        halo_spec = pl.BlockSpec((bb, HALO, d), lambda b, i, j: (b, jnp.maximum(i * per - 1, 0), 0))
    else:
        halo_arr = hist_pad
        halo_spec = pl.BlockSpec((bb, HALO, d), lambda b, i, j: (b, 0, 0))
    vec = lambda a: a.reshape(nl, 1, d)
    row_spec = pl.BlockSpec((None, 1, d), lambda b, i, j: (l, 0, 0))
    return pl.pallas_call(
        functools.partial(_conv_tail_kernel, rows_per_step=min(tt, 64)),
        out_shape=jax.ShapeDtypeStruct(x.shape, F32),
        grid=(bsz // bb, t // tt, d // tn),
        in_specs=[_x_spec(bb, tt, d),
                  halo_spec,
                  pl.BlockSpec((bb, HALO, d), lambda b, i, j: (b, 0, 0)),
                  pl.BlockSpec((bb, tt, tn), lambda b, i, j: (b, i, j)),
                  pl.BlockSpec((None, bb, None, 1, tn), lambda b, i, j: (l, b, 5, 0, j)),
                  pl.BlockSpec((None, CONV_W, d), lambda b, i, j: (l, 0, 0)),
                  row_spec, row_spec, row_spec,
                  pl.BlockSpec((None, d, tn), lambda b, i, j: (l, 0, j)),
                  pl.BlockSpec((None, 1, tn), lambda b, i, j: (l, 0, j))],
        out_specs=pl.BlockSpec((bb, tt, tn), lambda b, i, j: (b, i, j)),
        scratch_shapes=[pltpu.VMEM((bb, HALO + tt, d), F32),
                        pltpu.VMEM((bb * tt, d), BF if min(tt, 64) % 16 == 0 else F32)],
        compiler_params=_params("parallel", "parallel", "arbitrary"),
        name="conv_tail",
    )(u, halo_arr, hist_pad, x, mod5, dw, vec(dw_b), vec(ln_g), vec(ln_b), w_out, vec(b_out))


def _kv_kernel(x_ref, sh_ref, sc_ref, g_ref, w_ref, kg_ref, nf_ref, o_ref, h_ref):
    bb, tt, _ = x_ref.shape

    @pl.when(pl.program_id(2) == 0)
    def _():
        _modulate_into(h_ref, x_ref, g_ref, sh_ref, sc_ref)

    y = _dot(h_ref[...], w_ref[...])
    y = jnp.where(nf_ref[...] > 0.5, _rms_groups(y, kg_ref[...]), y)
    o_ref[...] = y.reshape(bb, tt, y.shape[-1])


def _kv_proj(x, kvmod5, kv_norm_g, w_kv, k_norm_g, bb, tt, tn):
    bsz, t, d = x.shape
    n = w_kv.shape[1]
    one = jnp.ones((KV_SLOT,), F32)
    zero = jnp.zeros((KV_SLOT,), F32)
    gains = jnp.concatenate([one, one, jnp.tile(k_norm_g[1], N_KV), one, jnp.tile(k_norm_g[2], N_KV), one])
    normed = jnp.concatenate([zero, zero, one, zero, one, zero])
    return pl.pallas_call(
        _kv_kernel,
        out_shape=jax.ShapeDtypeStruct((bsz, t, n), F32),
        grid=(bsz // bb, t // tt, n // tn),
        in_specs=[_x_spec(bb, tt, d),
                  _mod_spec(bb, d, 0, 0), _mod_spec(bb, d, 0, 1),
                  pl.BlockSpec((1, d), lambda b, i, j: (0, 0)),
                  pl.BlockSpec((d, tn), lambda b, i, j: (0, j)),
                  pl.BlockSpec((1, tn), lambda b, i, j: (0, j)),
                  pl.BlockSpec((1, tn), lambda b, i, j: (0, j))],
        out_specs=pl.BlockSpec((bb, tt, tn), lambda b, i, j: (b, i, j)),
        scratch_shapes=[pltpu.VMEM((bb * tt, d), BF)],
        compiler_params=_params("parallel", "parallel", "arbitrary"),
        name="kv_proj",
    )(x, kvmod5, kvmod5, kv_norm_g.reshape(1, d), w_kv, gains.reshape(1, n), normed.reshape(1, n))


def _q_kernel(x_ref, sh_ref, sc_ref, g_ref, wq_ref, wg_ref, qg_ref, q_ref, gate_ref, h_ref):
    j = pl.program_id(2)
    bb, tt, _ = x_ref.shape

    @pl.when(j == 0)
    def _():
        _modulate_into(h_ref, x_ref, g_ref, sh_ref, sc_ref)
        gl = _dot(h_ref[...], wg_ref[...])
        gate_ref[...] = _sigmoid(gl).reshape(bb, tt, gl.shape[-1])

    y = _dot(h_ref[...], wq_ref[...])
    q_ref[...] = _rms_groups(y, qg_ref[...]).reshape(bb, tt, y.shape[-1])


def _q_proj(x, mod5, norm4, w_q, w_gate, q_gain, l, lj, bb, tt, tn):
    bsz, t, d = x.shape
    hd = w_q.shape[2]
    ng = w_gate.shape[2]
    return pl.pallas_call(
        _q_kernel,
        out_shape=(jax.ShapeDtypeStruct((bsz, t, hd), F32), jax.ShapeDtypeStruct((bsz, t, ng), F32)),
        grid=(bsz // bb, t // tt, hd // tn),
        in_specs=[_x_spec(bb, tt, d),
                  _mod_spec(bb, d, l, 3), _mod_spec(bb, d, l, 4),
                  _gain_spec(d, l, 1),
                  pl.BlockSpec((None, d, tn), lambda b, i, j: (lj, 0, j)),
                  pl.BlockSpec((None, d, ng), lambda b, i, j: (lj, 0, 0)),
                  pl.BlockSpec((None, 1, tn), lambda b, i, j: (lj, 0, j))],
        out_specs=(pl.BlockSpec((bb, tt, tn), lambda b, i, j: (b, i, j)),
                   pl.BlockSpec((bb, tt, ng), lambda b, i, j: (b, i, 0))),
        scratch_shapes=[pltpu.VMEM((bb * tt, d), BF)],
        compiler_params=_params("parallel", "parallel", "arbitrary"),
        name="q_proj",
    )(x, mod5, mod5, norm4, w_q, w_gate, q_gain)


def _oproj_kernel(a_ref, x_ref, gt_ref, w_ref, o_ref):
    bb, tt, k = a_ref.shape
    out = _dot(a_ref[...].reshape(bb * tt, k).astype(BF), w_ref[...])
    o_ref[...] = x_ref[...] + gt_ref[...] * out.reshape(bb, tt, out.shape[-1])


def _o_proj(a, x, mod5, w_o, l, lj, bb, tt, tn):
    bsz, t, d = x.shape
    k = a.shape[2]
    return pl.pallas_call(
        _oproj_kernel,
        out_shape=jax.ShapeDtypeStruct(x.shape, F32),
        grid=(bsz // bb, t // tt, d // tn),
        in_specs=[_x_spec(bb, tt, k),
                  pl.BlockSpec((bb, tt, tn), lambda b, i, j: (b, i, j)),
                  pl.BlockSpec((None, bb, None, 1, tn), lambda b, i, j: (l, b, 5, 0, j)),
                  pl.BlockSpec((None, k, tn), lambda b, i, j: (lj, 0, j))],
        out_specs=pl.BlockSpec((bb, tt, tn), lambda b, i, j: (b, i, j)),
        compiler_params=_params("parallel", "parallel", "arbitrary"),
        name="o_proj",
    )(a, x, mod5, w_o)


def _compress_pair(rows_a, rows_b, j, pos_ref, w1_ref):
    lhs = jnp.concatenate(
        [jnp.concatenate([rows_a[g] + pos_ref[j:j + 1, :], rows_b[g] + pos_ref[j + 1:j + 2, :]], axis=-1)
         for g in range(N_KV)], axis=0)
    return _dot(lhs.astype(BF), w1_ref[j * HEAD_DIM:(j + 2) * HEAD_DIM, :])


def _compress_finish(acc, nb, b1_ref, w2_ref, kg_ref, is_key):
    hid = jax.nn.gelu(acc + b1_ref[...])
    out = _dot(hid.astype(BF), w2_ref[...].astype(BF))
    normed = out * lax.rsqrt(jnp.mean(out * out, axis=-1, keepdims=True) + EPS) * kg_ref[...]
    out = jnp.where(is_key, normed, out)
    return out.reshape(N_KV, nb, HEAD_DIM)


def _compress_prompt_kernel(r0_ref, r1_ref, r2_ref, r3_ref, pos_ref, w1_ref, b1_ref, w2_ref, kg_ref, o_ref):
    heads = [r0_ref, r1_ref, r2_ref, r3_ref]
    nb = r0_ref.shape[0] // BLK
    acc = jnp.zeros((N_KV * nb, w1_ref.shape[-1]), F32)
    for j in range(0, BLK, 2):
        acc = acc + _compress_pair([h[pl.ds(j, nb, stride=BLK), :] for h in heads],
                                   [h[pl.ds(j + 1, nb, stride=BLK), :] for h in heads], j, pos_ref, w1_ref)
    o_ref[...] = _compress_finish(acc, nb, b1_ref, w2_ref, kg_ref, pl.program_id(0) == 0)


def _compress_prompt(kv_all, pos_t, w1, b1, w2, kgain):
    bsz, t, _ = kv_all.shape
    nc = t // BLK
    kf = BLK * HEAD_DIM

    def head_spec(g):
        return pl.BlockSpec((None, nc * BLK, HEAD_DIM), lambda p, b: (b, 0, p * N_KV + g))

    return pl.pallas_call(
        _compress_prompt_kernel,
        out_shape=jax.ShapeDtypeStruct((2, bsz, N_KV, nc, HEAD_DIM), F32),
        grid=(2, bsz),
        in_specs=[head_spec(g) for g in range(N_KV)] + [
            pl.BlockSpec((None, BLK, HEAD_DIM), lambda p, b: (p, 0, 0)),
            pl.BlockSpec((None, kf, 2 * HEAD_DIM), lambda p, b: (p, 0, 0)),
            pl.BlockSpec((None, 1, 2 * HEAD_DIM), lambda p, b: (p, 0, 0)),
            pl.BlockSpec((None, 2 * HEAD_DIM, HEAD_DIM), lambda p, b: (p, 0, 0)),
            pl.BlockSpec((1, HEAD_DIM), lambda p, b: (0, 0))],
        out_specs=pl.BlockSpec((None, None, N_KV, nc, HEAD_DIM), lambda p, b: (p, b, 0, 0, 0)),
        compiler_params=_params("arbitrary", "arbitrary"),
        name="compress_prompt",
    )(kv_all, kv_all, kv_all, kv_all, pos_t, w1, b1.reshape(2, 1, -1), w2, kgain)


def _compress_paged_kernel(pt_ref, cache_ref, w1_hbm, pos_ref, b1_ref, w2_ref, kg_ref, o_ref,
                           rows_ref, w1_ref, sem, w1_sem):
    b = pl.program_id(0)
    s = pl.program_id(1)
    ns = pl.num_programs(1)
    step = b * ns + s
    n_steps = pl.num_programs(0) * ns
    page = cache_ref.shape[1]
    n_rows = rows_ref.shape[0] // 2
    n_pg = n_rows // page
    nb = n_rows // BLK
    n_sub = 2 * N_KV
    slot = step % 2
    base = pl.multiple_of(slot * n_rows, n_rows)

    def page_copy(bi, si, k, sl):
        src = cache_ref.at[pt_ref[bi, si * n_pg + k], :, pl.ds(0, n_sub), :]
        dst = rows_ref.at[pl.ds(pl.multiple_of(sl * n_rows + k * page, page), page), :, :]
        return pltpu.make_async_copy(src, dst, sem.at[sl])

    def start_step(bi, si, sl):
        def body(k, c):
            page_copy(bi, si, k, sl).start()
            return c
        lax.fori_loop(0, n_pg, body, 0)

    @pl.when(step == 0)
    def _():
        w1_copy = pltpu.make_async_copy(w1_hbm, w1_ref, w1_sem)
        w1_copy.start()
        start_step(b, s, slot)
        w1_copy.wait()

    @pl.when(step + 1 < n_steps)
    def _():
        nxt = step + 1
        start_step(nxt // ns, nxt % ns, 1 - slot)

    def wait_body(k, c):
        page_copy(b, s, k, slot).wait()
        return c

    lax.fori_loop(0, n_pg, wait_body, 0)

    def block_rows(j):
        return jnp.swapaxes(rows_ref[pl.ds(base + j, nb, stride=BLK), :, :], 0, 1)

    accs = [jnp.zeros((N_KV * nb, w1_ref.shape[-1]), F32) for _ in range(2)]
    for j in range(0, BLK, 2):
        xa = block_rows(j)
        xb = block_rows(j + 1)
        for p in range(2):
            heads = range(p * N_KV, (p + 1) * N_KV)
            accs[p] = accs[p] + _compress_pair([xa[h] for h in heads], [xb[h] for h in heads], j,
                                               pos_ref.at[p], w1_ref.at[p])
    for p in range(2):
        o_ref[p] = _compress_finish(accs[p], nb, b1_ref.at[p], w2_ref.at[p], kg_ref, p == 0)


def _compress_paged(cache4, page_table, pos_t, w1, b1, w2, kgain, pages_per_step):
    bsz, n_pages = page_table.shape
    page = cache4.shape[1]
    nb = pages_per_step * page // BLK
    nc = n_pages * page // BLK
    full = lambda a: pl.BlockSpec(a.shape, lambda b, s, pt: (0,) * a.ndim)
    b1r = b1.reshape(2, 1, -1)
    grid_spec = pltpu.PrefetchScalarGridSpec(
        num_scalar_prefetch=1,
        grid=(bsz, n_pages // pages_per_step),
        in_specs=[pl.BlockSpec(memory_space=pl.ANY), pl.BlockSpec(memory_space=pl.ANY),
                  full(pos_t), full(b1r), full(w2), full(kgain)],
        out_specs=pl.BlockSpec((2, None, N_KV, nb, HEAD_DIM), lambda b, s, pt: (0, b, 0, s, 0)),
        scratch_shapes=[pltpu.VMEM((2 * pages_per_step * page, 2 * N_KV, HEAD_DIM), F32),
                        pltpu.VMEM(w1.shape, w1.dtype),
                        pltpu.SemaphoreType.DMA((2,)), pltpu.SemaphoreType.DMA],
    )
    return pl.pallas_call(
        _compress_paged_kernel,
        out_shape=jax.ShapeDtypeStruct((2, bsz, N_KV, nc, HEAD_DIM), F32),
        grid_spec=grid_spec,
        compiler_params=_params("arbitrary", "arbitrary"),
        name="compress_paged",
    )(page_table, cache4, w1, pos_t, b1r, w2, kgain)


def _stack_heads(q_val):
    return jnp.concatenate([q_val[:, r * HEAD_DIM:(r + 1) * HEAD_DIM] for r in range(GROUP)], axis=0)


def _masked_scores(q, k_bf, valid, dist, slopes_ref, g):
    t = dist.shape[0]
    raw = _dot_nt(q, k_bf)
    return jnp.concatenate(
        [jnp.where(valid, raw[r * t:(r + 1) * t] * SCALE - slopes_ref[g, r] * dist, NEG) for r in range(GROUP)],
        axis=0)


def _softmax_full(s):
    m = jnp.maximum(jnp.max(s, axis=-1, keepdims=True), M_FLOOR)
    e = jnp.exp(s - m)
    return e, jnp.sum(e, axis=-1, keepdims=True)


def _online_update(s, v_bf, m_old, l_old, acc_old):
    m_new = jnp.maximum(m_old, jnp.max(s, axis=-1, keepdims=True))
    e = jnp.exp(s - jnp.maximum(m_new, M_FLOOR))
    alpha = jnp.exp(m_old - m_new)
    l_new = alpha * l_old + jnp.sum(e, axis=-1, keepdims=True)
    acc_new = alpha * acc_old + _dot(e.astype(BF), v_bf)
    return m_new, l_new, acc_new


def _head_sum(p, t):
    out = p[0:t]
    for r in range(1, GROUP):
        out = out + p[r * t:(r + 1) * t]
    return out


def _top_blocks(score, blk, k):
    n = score.shape[-1]
    sel = jnp.zeros(score.shape, F32)
    for _ in range(k):
        mx = jnp.max(score, axis=-1, keepdims=True)
        idx = jnp.min(jnp.where(score == mx, blk, float(n)), axis=-1, keepdims=True)
        hit = blk == idx
        sel = jnp.where(hit & (mx >= 0.0), 1.0, sel)
        score = jnp.where(hit, -2.0, score)
    return sel


def _rank_select_t(score_t, k):
    n, t = score_t.shape
    groups = [score_t[r0:r0 + SUBLANES] for r0 in range(0, n, SUBLANES)]
    ranks = [jnp.zeros((SUBLANES, t), F32) for _ in groups]
    sub = lax.broadcasted_iota(jnp.int32, (SUBLANES, t), 0)
    for j in range(n):
        sj = score_t[j:j + 1, :]
        for gi, grp in enumerate(groups):
            lo = gi * SUBLANES
            if lo > j:
                beats = jnp.where(sj >= grp, 1.0, 0.0)
            elif lo + SUBLANES - 1 <= j:
                beats = jnp.where(sj > grp, 1.0, 0.0)
            else:
                beats = jnp.where(sub > j - lo, jnp.where(sj >= grp, 1.0, 0.0), jnp.where(sj > grp, 1.0, 0.0))
            ranks[gi] = ranks[gi] + beats
    rank = jnp.concatenate(ranks, axis=0)
    return jnp.where((rank < float(k)) & (score_t >= 0.0), 1.0, 0.0)


def _expand_matrix(nsel, first_blk, n_keys):
    row = lax.broadcasted_iota(jnp.int32, (nsel, n_keys), 0)
    col = lax.broadcasted_iota(jnp.int32, (nsel, n_keys), 1)
    return jnp.where(row == first_blk + col // BLK, 1.0, 0.0).astype(BF)


def _combine(gate, o_cmp, o_sel, o_win, t):
    cols = []
    for r in range(GROUP):
        rs = slice(r * t, (r + 1) * t)
        cols.append(gate[:, r:r + 1] * o_cmp[rs] + gate[:, GROUP + r:GROUP + r + 1] * o_sel[rs]
                    + gate[:, 2 * GROUP + r:2 * GROUP + r + 1] * o_win[rs])
    return jnp.concatenate(cols, axis=-1)


def _nsa_prompt_kernel(slopes_ref, q_ref, gate_ref, kc_ref, vc_ref, ks_ref, vs_ref, kw_ref, vw_ref, relb_ref,
                       o_ref, m_ref, l_ref, acc_ref, any_ref, *, tk):
    g = pl.program_id(1)
    qi = pl.program_id(2)
    tq = q_ref.shape[0]
    nc = kc_ref.shape[0]
    rows = GROUP * tq
    t0 = qi * tq
    q = _stack_heads(q_ref[...]).astype(BF)
    c1 = SCALE * LOG2E

    blk_t = lax.broadcasted_iota(jnp.int32, (nc, tq), 0)
    tok_t = t0 + lax.broadcasted_iota(jnp.int32, (nc, tq), 1)
    dist_t = (tok_t - (blk_t * BLK + (BLK - 1))).astype(F32)
    valid_t = dist_t >= 0
    raw_t = _dot_nt(kc_ref[...].astype(BF), q)
    p_heads = []
    for r in range(GROUP):
        s = jnp.where(valid_t, raw_t[:, r * tq:(r + 1) * tq] * SCALE - slopes_ref[g, r] * dist_t, NEG)
        m = jnp.maximum(jnp.max(s, axis=0, keepdims=True), M_FLOOR)
        e = jnp.exp(s - m)
        p_heads.append(e / jnp.maximum(jnp.sum(e, axis=0, keepdims=True), 1e-30))
    o_cmp = _dot_tn(jnp.concatenate(p_heads, axis=1).astype(BF), vc_ref[...].astype(BF))
    imp_t = p_heads[0]
    for r in range(1, GROUP):
        imp_t = imp_t + p_heads[r]

    cur_t = tok_t // BLK
    forced = (blk_t == 0) | (blk_t == cur_t) | (blk_t == cur_t - 1)
    score_t = jnp.where(forced, FORCE, imp_t)
    score_t = jnp.where(blk_t <= cur_t, score_t, -1.0)
    sel_t = _rank_select_t(score_t, min(TOP_N, nc))
    unsel = ((1.0 - sel_t) * BIG).T.astype(BF)

    m_ref[...] = jnp.full(m_ref.shape, NEG, F32)
    l_ref[...] = jnp.zeros(l_ref.shape, F32)
    acc_ref[...] = jnp.zeros(acc_ref.shape, F32)
    rel = lax.broadcasted_iota(jnp.int32, (tq, tk), 0) - lax.broadcasted_iota(jnp.int32, (tq, tk), 1)
    slope_col = jnp.concatenate([jnp.full((tq, 1), slopes_ref[g, r] * LOG2E, F32) for r in range(GROUP)], axis=0)

    bpt = tk // BLK
    blk_any = jnp.max(sel_t, axis=1, keepdims=True)
    for kt in range(nc // bpt):
        any_ref[kt] = jnp.max(blk_any[kt * bpt:(kt + 1) * bpt])

    def scores(raw, mask, width):
        return jnp.concatenate([raw[r * tq:(r + 1) * tq] * c1 - relb_ref[r * tq:(r + 1) * tq, 0:width] - mask
                                for r in range(GROUP)], axis=0)

    def early_tile(kt, carry):
        @pl.when(any_ref[kt] > 0.5)
        def _():
            k0 = pl.multiple_of(kt * tk, tk)
            raw = _dot_nt(q, ks_ref[pl.ds(k0, tk), :].astype(BF))
            s = scores(raw, _dot(unsel, _expand_matrix(nc, kt * bpt, tk)), tk)
            col = slope_col * (t0 - k0).astype(F32)
            m_old = m_ref[...]
            m_new = jnp.maximum(m_old, jnp.max(s, axis=-1, keepdims=True) - col)
            e = jnp.exp2(s - (jnp.maximum(m_new, M_FLOOR) + col))
            alpha = jnp.exp2(m_old - m_new)
            l_ref[...] = alpha * l_ref[...] + jnp.sum(e, axis=-1, keepdims=True)
            acc_ref[...] = alpha * acc_ref[...] + _dot(e.astype(BF), vs_ref[pl.ds(k0, tk), :].astype(BF))
            m_ref[...] = m_new
        return carry

    k_diag = t0 // tk
    lax.fori_loop(0, k_diag, early_tile, 0)

    k0 = pl.multiple_of(k_diag * tk, tk)
    lw = WINDOW + tq
    w0 = pl.multiple_of(jnp.maximum(t0 - WINDOW, 0), SUBLANES)
    raw_d = _dot_nt(q, ks_ref[pl.ds(k0, tk), :].astype(BF))
    raw_w = _dot_nt(q, kw_ref[pl.ds(w0, lw), :].astype(BF))
    mask_d = _dot(unsel, _expand_matrix(nc, k_diag * bpt, tk)) + jnp.where(rel + (t0 - k0) >= 0, 0.0, BIG)
    dist_w = (t0 - w0) + (lax.broadcasted_iota(jnp.int32, (tq, lw), 0) - lax.broadcasted_iota(jnp.int32, (tq, lw), 1))
    mask_w = jnp.where((dist_w >= 0) & (dist_w < WINDOW), 0.0, BIG)
    s_d = scores(raw_d, mask_d, tk)
    s_w = scores(raw_w, mask_w, lw)
    col = slope_col * (t0 - k0).astype(F32)
    m_old = m_ref[...]
    m_new = jnp.maximum(m_old, jnp.max(s_d, axis=-1, keepdims=True) - col)
    m_w = jnp.maximum(jnp.max(s_w, axis=-1, keepdims=True), M_FLOOR)
    e_d = jnp.exp2(s_d - (jnp.maximum(m_new, M_FLOOR) + col))
    e_w = jnp.exp2(s_w - m_w)
    alpha = jnp.exp2(m_old - m_new)
    l_d = alpha * l_ref[...] + jnp.sum(e_d, axis=-1, keepdims=True)
    l_w = jnp.sum(e_w, axis=-1, keepdims=True)
    acc_d = alpha * acc_ref[...] + _dot(e_d.astype(BF), vs_ref[pl.ds(k0, tk), :].astype(BF))
    acc_w = _dot(e_w.astype(BF), vw_ref[pl.ds(w0, lw), :].astype(BF))
    o_sel = acc_d / jnp.maximum(l_d, 1e-30)
    o_win = acc_w / jnp.maximum(l_w, 1e-30)

    o_ref[...] = _combine(gate_ref[...], o_cmp, o_sel, o_win, tq)


def _nsa_prompt(q, gates, cmp_kv, kv_all, slopes, tq, tk):
    bsz, t, hd = q.shape
    nc = cmp_kv.shape[3]
    lw = WINDOW + tq
    assert t % BLK == 0 and t % tk == 0 and t >= lw and tk <= lw and tk % tq == 0 and tk % BLK == 0
    rows = GROUP * tq
    rel = (jnp.arange(tq, dtype=F32)[:, None] - jnp.arange(lw, dtype=F32)[None, :])
    relb = ((slopes * LOG2E)[:, :, None, None] * rel[None, None]).reshape(N_KV, rows, lw)

    def col(slot):
        return pl.BlockSpec((None, t, HEAD_DIM), lambda b, g, i: (b, 0, slot * N_KV + g))

    def cmp_spec(p):
        return pl.BlockSpec((None, None, None, nc, HEAD_DIM), lambda b, g, i: (p, b, g, 0, 0))

    return pl.pallas_call(
        functools.partial(_nsa_prompt_kernel, tk=tk),
        out_shape=jax.ShapeDtypeStruct(q.shape, F32),
        grid=(bsz, N_KV, t // tq),
        in_specs=[pl.BlockSpec(memory_space=pltpu.SMEM),
                  pl.BlockSpec((None, tq, GROUP * HEAD_DIM), lambda b, g, i: (b, i, g)),
                  pl.BlockSpec((None, tq, HEAD_DIM), lambda b, g, i: (b, i, g)),
                  cmp_spec(0), cmp_spec(1), col(2), col(3), col(4), col(5),
                  pl.BlockSpec((None, rows, lw), lambda b, g, i: (g, 0, 0))],
        out_specs=pl.BlockSpec((None, tq, GROUP * HEAD_DIM), lambda b, g, i: (b, i, g)),
        scratch_shapes=[pltpu.VMEM((rows, 1), F32), pltpu.VMEM((rows, 1), F32),
                        pltpu.VMEM((rows, HEAD_DIM), F32), pltpu.SMEM((t // tk,), F32)],
        compiler_params=_params("parallel", "parallel", "arbitrary"),
        name="nsa_prompt",
    )(slopes, q, gates, cmp_kv, cmp_kv, kv_all, kv_all, kv_all, kv_all, relb)


def _nsa_paged_kernel(pt_ref, slopes_ref, q_ref, gate_ref, kc_ref, vc_ref, new_ref, cwin_ref, *rest,
                      n_pg, past_len):
    page_refs = rest[:n_pg]
    o_ref, sel_ref, m_ref, l_ref, acc_ref, ocmp_ref, owin_ref, pad_ref = rest[n_pg:]
    s_idx = pl.program_id(1)
    tq = q_ref.shape[0]
    nc = kc_ref.shape[1]
    page = page_refs[0].shape[0]
    rows = GROUP * tq
    npad = pad_ref.shape[0]
    wlen = cwin_ref.shape[0]

    def q_rows(g):
        return _stack_heads(q_ref[:, g * KV_SLOT:(g + 1) * KV_SLOT]).astype(BF)

    def tok_pos(n):
        return past_len + lax.broadcasted_iota(jnp.int32, (tq, n), 0)

    def new_cols(slot, g):
        return pad_ref[:, slot * KV_SLOT + g * HEAD_DIM:slot * KV_SLOT + (g + 1) * HEAD_DIM]

    def by_head(ref):
        return jnp.swapaxes(ref[...], 0, 1)

    @pl.when(s_idx == 0)
    def _():
        pad_ref[...] = jnp.zeros(pad_ref.shape, F32)
        pad_ref[0:tq, :] = new_ref[...]
        new_i = lax.broadcasted_iota(jnp.int32, (tq, npad), 1)
        blk_i = lax.broadcasted_iota(jnp.int32, (tq, nc), 1)
        cwin = by_head(cwin_ref)
        imps = []
        for g in range(N_KV):
            q = q_rows(g)
            dist = (tok_pos(nc) - (blk_i * BLK + (BLK - 1))).astype(F32)
            e, l = _softmax_full(_masked_scores(q, kc_ref[g].astype(BF), dist >= 0, dist, slopes_ref, g))
            p_cmp = e / jnp.maximum(l, 1e-30)
            ocmp_ref[g] = _dot(p_cmp.astype(BF), vc_ref[g].astype(BF))
            imps.append(_head_sum(p_cmp, tq))
            dist = (tok_pos(npad) - (past_len + new_i)).astype(F32)
            valid = (dist >= 0) & (new_i < tq)
            s = _masked_scores(q, new_cols(2, g).astype(BF), valid, dist, slopes_ref, g)
            m, l, acc = _online_update(s, new_cols(3, g).astype(BF), jnp.full((rows, 1), NEG, F32),
                                       jnp.zeros((rows, 1), F32), jnp.zeros((rows, HEAD_DIM), F32))
            m_ref[g] = m
            l_ref[g] = l
            acc_ref[g] = acc
            kw = jnp.concatenate([cwin[g], new_cols(4, g)], axis=0)
            vw = jnp.concatenate([cwin[N_KV + g], new_cols(5, g)], axis=0)
            win_i = lax.broadcasted_iota(jnp.int32, (tq, wlen + npad), 1)
            dist = (tok_pos(wlen + npad) - (past_len - wlen + win_i)).astype(F32)
            valid = (dist >= 0) & (dist < WINDOW) & (win_i < wlen + tq)
            e, l = _softmax_full(_masked_scores(q, kw.astype(BF), valid, dist, slopes_ref, g))
            owin_ref[g] = _dot(e.astype(BF), vw.astype(BF)) / jnp.maximum(l, 1e-30)
        blk_all = lax.broadcasted_iota(jnp.int32, (N_KV * tq, nc), 1)
        score = jnp.where((blk_all == 0) | (blk_all == nc - 1), FORCE, jnp.concatenate(imps, axis=0))
        sel = _top_blocks(score, blk_all.astype(F32), min(TOP_N, nc + 1) - 1)
        for g in range(N_KV):
            sel_ref[g] = sel[g * tq:(g + 1) * tq]

    nk = n_pg * page
    k0 = s_idx * nk
    kpos = k0 + lax.broadcasted_iota(jnp.int32, (tq, nk), 1)
    dist = (tok_pos(nk) - kpos).astype(F32)
    expand = _expand_matrix(nc, s_idx * (nk // BLK), nk)
    pages = [by_head(pr) for pr in page_refs]
    scores = []
    for g in range(N_KV):
        ks = jnp.concatenate([pg[g] for pg in pages], axis=0)
        valid = (_dot(sel_ref[g].astype(BF), expand) > 0.5) & (dist >= 0)
        scores.append(_masked_scores(q_rows(g), ks.astype(BF), valid, dist, slopes_ref, g))
    s = jnp.concatenate(scores, axis=0)
    m_old = m_ref[...].reshape(N_KV * rows, 1)
    m_new = jnp.maximum(m_old, jnp.max(s, axis=-1, keepdims=True))
    e = jnp.exp(s - jnp.maximum(m_new, M_FLOOR))
    alpha = jnp.exp(m_old - m_new)
    l_new = alpha * l_ref[...].reshape(N_KV * rows, 1) + jnp.sum(e, axis=-1, keepdims=True)
    m_ref[...] = m_new.reshape(N_KV, rows, 1)
    l_ref[...] = l_new.reshape(N_KV, rows, 1)
    e = e.astype(BF)
    for g in range(N_KV):
        rs = slice(g * rows, (g + 1) * rows)
        vs = jnp.concatenate([pg[N_KV + g] for pg in pages], axis=0)
        acc_ref[g] = alpha[rs] * acc_ref[g] + _dot(e[rs], vs.astype(BF))

    @pl.when(s_idx == pl.num_programs(1) - 1)
    def _():
        outs = []
        for g in range(N_KV):
            o_sel = acc_ref[g] / jnp.maximum(l_ref[g], 1e-30)
            outs.append(_combine(gate_ref[:, g * HEAD_DIM:(g + 1) * HEAD_DIM], ocmp_ref[g], o_sel, owin_ref[g], tq))
        o_ref[...] = jnp.concatenate(outs, axis=-1)


def _nsa_paged(q, gates, cmp_kv, kv_new, cache4, cache_win3, page_table, slopes, n_pg):
    bsz, tq, hd = q.shape
    n_pages = page_table.shape[1]
    page = cache4.shape[1]
    past_len = n_pages * page
    nc = cmp_kv.shape[3]
    wlen = cache_win3.shape[1]
    assert past_len % BLK == 0 and tq < BLK and nc == past_len // BLK and n_pages % n_pg == 0
    assert wlen == WINDOW and tq % SUBLANES == 0
    rows = GROUP * tq
    npad = 128

    def cmp_spec(p):
        return pl.BlockSpec((None, None, N_KV, nc, HEAD_DIM), lambda b, s, pt: (p, b, 0, 0, 0))

    def page_spec(k):
        return pl.BlockSpec((None, page, 2 * N_KV, HEAD_DIM), lambda b, s, pt: (pt[b, s * n_pg + k], 0, 1, 0))

    grid_spec = pltpu.PrefetchScalarGridSpec(
        num_scalar_prefetch=1,
        grid=(bsz, n_pages // n_pg),
        in_specs=[pl.BlockSpec(memory_space=pltpu.SMEM),
                  pl.BlockSpec((None, tq, hd), lambda b, s, pt: (b, 0, 0)),
                  pl.BlockSpec((None, tq, gates.shape[2]), lambda b, s, pt: (b, 0, 0)),
                  cmp_spec(0), cmp_spec(1),
                  pl.BlockSpec((None, tq, kv_new.shape[2]), lambda b, s, pt: (b, 0, 0)),
                  pl.BlockSpec((None, wlen, 2 * N_KV, HEAD_DIM), lambda b, s, pt: (b, 0, 0, 0))]
                 + [page_spec(k) for k in range(n_pg)],
        out_specs=pl.BlockSpec((None, tq, hd), lambda b, s, pt: (b, 0, 0)),
        scratch_shapes=[pltpu.VMEM((N_KV, tq, nc), F32),
                        pltpu.VMEM((N_KV, rows, 1), F32), pltpu.VMEM((N_KV, rows, 1), F32),
                        pltpu.VMEM((N_KV, rows, HEAD_DIM), F32),
                        pltpu.VMEM((N_KV, rows, HEAD_DIM), F32), pltpu.VMEM((N_KV, rows, HEAD_DIM), F32),
                        pltpu.VMEM((npad, kv_new.shape[2]), F32)],
    )
    return pl.pallas_call(
        functools.partial(_nsa_paged_kernel, n_pg=n_pg, past_len=past_len),
        out_shape=jax.ShapeDtypeStruct(q.shape, F32),
        grid_spec=grid_spec,
        compiler_params=_params("parallel", "arbitrary"),
        name="nsa_paged",
    )(page_table, slopes, q, gates, cmp_kv, cmp_kv, kv_new, cache_win3, *([cache4] * n_pg))


def _alibi_slopes():
    n_heads = N_KV * GROUP
    m = np.exp2(-8.0 * np.arange(1, n_heads + 1) / n_heads)
    return jnp.asarray(m, dtype=F32).reshape(N_KV, GROUP)


def _gate_weights(w_qg, hd):
    nl, d, _ = w_qg.shape
    wg = w_qg[:, :, hd:].reshape(nl, d, 3, N_KV, GROUP).transpose(0, 1, 3, 2, 4).reshape(nl, d, N_KV, 3 * GROUP)
    wg = jnp.pad(wg, ((0, 0), (0, 0), (0, 0), (0, HEAD_DIM - 3 * GROUP)))
    return wg.reshape(nl, d, N_KV * HEAD_DIM)


def _forward(x, mod5, kvmod5, hist, paged, p, tiles):
    bsz, t, d = x.shape
    bb, tt, tm, tn = tiles["bb"], tiles["tt"], tiles["tm"], tiles["tn"]
    n_a = p["conv_w_in"].shape[0]
    depth = p["ffn_w_in"].shape[0]
    norm4 = p["norm_g"].reshape(depth, 3, 1, d)
    new_hist = []
    kv_all = None
    cmp_kv = None

    def ffn(x, l, sub):
        return _ffn(x, mod5, norm4, p["ffn_w_in"], p["ffn_w_out"], l, sub, 6 * sub, 2 * sub, bb, tm,
                    tiles["tf"], tiles["tn_ffn"])

    for l in range(depth):
        if l == n_a:
            kv_all = _kv_proj(x, kvmod5, p["kv_norm_g"], p["w_kv"], p["k_norm_g"], bb, tm, tn)
            if paged is None:
                cmp_kv = _compress_prompt(kv_all, p["pos_t"], p["cmp_w1"], p["cmp_b1"], p["cmp_w2"], p["kgain0"])
            else:
                cmp_kv = _compress_paged(paged["cache4"], paged["page_table"], p["pos_t"], p["cmp_w1"],
                                         p["cmp_b1"], p["cmp_w2"], p["kgain0"], tiles["cmp_pages"])
        x = ffn(x, l, 0)
        if l < n_a:
            u = _glu(x, mod5, norm4, p["conv_w_in"], p["conv_b_in"], l, bb, tm, tn)
            hist_pad = jnp.pad(hist[l], ((0, 0), (HALO - (CONV_W - 1), 0), (0, 0)))
            x = _conv_tail(u, hist_pad, x, mod5, p["conv_dw"], p["conv_dw_b"], p["conv_ln_g"], p["conv_ln_b"],
                           p["conv_w_out"], p["conv_b_out"], l, bb, tt, tn)
            new_hist.append(jnp.concatenate([hist[l], u], axis=1)[:, -(CONV_W - 1):])
        else:
            lj = l - n_a
            q, gates = _q_proj(x, mod5, norm4, p["w_q"], p["w_gate"], p["q_gain"], l, lj, bb, tm, tn)
            if paged is None:
                a = _nsa_prompt(q, gates, cmp_kv, kv_all, p["slopes"], tiles["tq"], tiles["tk"])
            else:
                a = _nsa_paged(q, gates, cmp_kv, kv_all, paged["cache4"], paged["cache_win3"],
                               paged["page_table"], p["slopes"], tiles["nsa_pages"])
            x = _o_proj(a, x, mod5, p["w_o"], l, lj, bb, tm, tn)
        x = ffn(x, l, 1)
    return x, kv_all, jnp.stack(new_hist)


def kernel(x_prompt, x_sample, c_prompt, c_sample, cache_kv, cache_win, state_conv, page_table, ada_w, ada_b, norm_g, ffn_w_in, ffn_w_out, conv_w_in, conv_b_in, conv_dw, conv_dw_b, conv_ln_g, conv_ln_b, conv_w_out, conv_b_out, kv_norm_g, kv_ada_w, kv_ada_b, w_kv, cmp_pos, cmp_w1, cmp_b1, cmp_w2, k_norm_g, w_qg, q_norm_g, w_o):
    bp, tp, d = x_prompt.shape
    bs, ts, _ = x_sample.shape
    depth = ada_w.shape[0]
    hd = N_KV * GROUP * HEAD_DIM
    n_b = w_qg.shape[0]

    n_c = bp + bs
    c_all = jnp.pad(jnp.concatenate([c_prompt, c_sample], axis=0), ((0, -n_c % 16), (0, 0)))
    mod = _ada(c_all, ada_w, ada_b, 1024).reshape(depth, c_all.shape[0], 9, 1, d)
    kvmod = _ada(c_all, kv_ada_w[None], kv_ada_b[None], 1024).reshape(1, c_all.shape[0], 2, 1, d)

    p = dict(norm_g=norm_g, ffn_w_in=ffn_w_in.astype(BF), ffn_w_out=ffn_w_out.astype(BF),
             conv_w_in=conv_w_in.astype(BF), conv_b_in=conv_b_in, conv_dw=conv_dw, conv_dw_b=conv_dw_b,
             conv_ln_g=conv_ln_g, conv_ln_b=conv_ln_b, conv_w_out=conv_w_out.astype(BF), conv_b_out=conv_b_out,
             kv_norm_g=kv_norm_g, w_kv=w_kv.astype(BF), k_norm_g=k_norm_g,
             cmp_w1=cmp_w1.astype(BF), cmp_b1=cmp_b1, cmp_w2=cmp_w2,
             pos_t=jnp.transpose(cmp_pos, (1, 0, 2)),
             kgain0=k_norm_g[0].reshape(1, HEAD_DIM),
             w_q=w_qg[:, :, :hd].astype(BF), w_gate=_gate_weights(w_qg, hd).astype(BF),
             q_gain=jnp.tile(q_norm_g, (1, N_KV * GROUP)).reshape(n_b, 1, hd),
             w_o=w_o.astype(BF), slopes=_alibi_slopes())

    hist0 = jnp.zeros((conv_dw.shape[0], bp, CONV_W - 1, d), F32)
    tiles_p, tiles_s = _tile_plans(tp, ts, bs, ffn_w_out.shape[2], page_table.shape[1])
    y_prompt, kv_p, conv_prompt = _forward(x_prompt, mod[:, :bp], kvmod[:, :bp], hist0, None, p, tiles_p)
    kv_prompt = kv_p[:, :, :4 * KV_SLOT].reshape(bp, tp, 4, N_KV, HEAD_DIM)
    wk = min(WINDOW, tp)
    win_prompt = kv_p[:, tp - wk:, 4 * KV_SLOT:].reshape(bp, wk, 2, N_KV, HEAD_DIM)

    n_pool, page = cache_kv.shape[0], cache_kv.shape[1]
    wlen = cache_win.shape[1]
    paged = dict(cache4=cache_kv.reshape(n_pool, page, 4 * N_KV, HEAD_DIM),
                 cache_win3=cache_win.reshape(bs, wlen, 2 * N_KV, HEAD_DIM),
                 page_table=page_table)
    y_sample, kv_s, conv_sample = _forward(x_sample, mod[:, bp:n_c], kvmod[:, bp:n_c], state_conv, paged, p, tiles_s)
    kv_sample = kv_s[:, :, :4 * KV_SLOT].reshape(bs, ts, 4, N_KV, HEAD_DIM)
    win_new = kv_s[:, :, 4 * KV_SLOT:].reshape(bs, ts, 2, N_KV, HEAD_DIM)
    win_sample = jnp.concatenate([cache_win, win_new], axis=1)[:, -wlen:]
    return (y_prompt, y_sample, kv_prompt, kv_sample, win_prompt, win_sample, conv_prompt, conv_sample)
```
